```python
import jax, jax.numpy as jnp
from jax import lax
import numpy as np

D_MODEL = 2048
BATCH = 2
SEQ = 4096
DEPTH = 1
DEC_BATCH = 2
DEC_SEQ = 8192
PAST_LEN = 128

N_META = 16
GRID_W = 64
MAX_KH = 8
KW = 16
D_POOL = D_MODEL // 2
D_ATTN = D_MODEL - D_POOL
N_POOL_GROUPS = 4
POOL_WINDOWS = (2, 4, 8, 16)
POOL_GROUP_DIM = D_POOL // N_POOL_GROUPS
HEAD_DIM = 64
N_HEADS = D_ATTN // HEAD_DIM
D_FF = ((8 * D_MODEL // 3 + 127) // 128) * 128
CONV_W = 3
LN_EPS = 1e-5
RPB_ROWS = 2 * MAX_KH - 1
RPB_COLS = 2 * KW - 1
DEEPNORM_ALPHA = float((2 * DEPTH) ** 0.25)
DEEPNORM_BETA = float((8 * DEPTH) ** -0.25)

kernel_name = "hybrid_pool_natten_encoder"


def layernorm(x, g, b):
    xf = x.astype(jnp.float32)
    mu = jnp.mean(xf, axis=-1, keepdims=True)
    var = jnp.mean(jnp.square(xf - mu), axis=-1, keepdims=True)
    y = (xf - mu) * lax.rsqrt(var + LN_EPS) * g.astype(jnp.float32) + b.astype(jnp.float32)
    return y.astype(x.dtype)


def multi_scale_pool(u, w_pool, pool_scale):
    B, L, _ = u.shape
    uf = u.astype(jnp.float32)
    cs = jnp.concatenate([jnp.zeros((B, 1, D_POOL), jnp.float32), jnp.cumsum(uf, axis=1)], axis=1)
    t = jnp.arange(L)
    diffs = []
    for g, w in enumerate(POOL_WINDOWS):
        sl = slice(g * POOL_GROUP_DIM, (g + 1) * POOL_GROUP_DIM)
        lo = jnp.clip(t - w // 2, 0, L)
        hi = jnp.clip(t - w // 2 + w, 0, L)
        cnt = (hi - lo).astype(jnp.float32)[None, :, None]
        mean = (cs[:, hi, sl] - cs[:, lo, sl]) / cnt
        diffs.append(mean - uf[:, :, sl])
    m = jnp.stack(diffs, axis=2).astype(u.dtype)
    y = jnp.einsum('blgc,gce->blge', m, w_pool).reshape(B, L, D_POOL)
    return y * pool_scale


def _softmax_split(s_loc, s_meta):
    n_loc = s_loc.shape[-1]
    p = jax.nn.softmax(jnp.concatenate([s_loc, s_meta], axis=-1), axis=-1)
    return p[..., :n_loc], p[..., n_loc:]


def neighbourhood_attention(q, k, v, rpb, meta_bias):
    B, L, H, Dh = q.shape
    T = L - N_META
    rows = T // GRID_W
    kh = min(MAX_KH, rows)
    scale = HEAD_DIM ** -0.5
    f32 = jnp.float32
    rpb = rpb.astype(f32)
    meta_b = meta_bias.astype(f32)[None, :, None, :]

    qm, km, vm = q[:, :N_META], k[:, :N_META], v[:, :N_META]
    qg = q[:, N_META:].reshape(B, rows, GRID_W, H, Dh)
    kg = k[:, N_META:].reshape(B, rows, GRID_W, H, Dh)
    vg = v[:, N_META:].reshape(B, rows, GRID_W, H, Dh)

    cols = jnp.arange(GRID_W)
    col_start = jnp.clip(cols - KW // 2, 0, GRID_W - KW)
    col_idx = col_start[:, None] + jnp.arange(KW)[None, :]
    dc_idx = col_idx - cols[:, None] + (KW - 1)

    def row_block(r):
        rs = jnp.clip(r - kh // 2, 0, rows - kh)
        k_rows = lax.dynamic_slice_in_dim(kg, rs, kh, axis=1)
        v_rows = lax.dynamic_slice_in_dim(vg, rs, kh, axis=1)
        k_win = k_rows[:, :, col_idx]
        v_win = v_rows[:, :, col_idx]
        q_row = lax.dynamic_index_in_dim(qg, r, axis=1, keepdims=False)
        dr_idx = rs + jnp.arange(kh) - r + (MAX_KH - 1)
        bias = rpb[:, dr_idx[None, :, None], dc_idx[:, None, :]]
        s_loc = jnp.einsum('bchd,bkcwhd->bhckw', q_row, k_win).astype(f32) * scale + bias[None]
        s_meta = jnp.einsum('bchd,bmhd->bhcm', q_row, km).astype(f32) * scale + meta_b
        p_loc, p_meta = _softmax_split(s_loc.reshape(B, H, GRID_W, kh * KW), s_meta)
        p_loc = p_loc.reshape(B, H, GRID_W, kh, KW).astype(v.dtype)
        out = (jnp.einsum('bhckw,bkcwhd->bchd', p_loc, v_win)
               + jnp.einsum('bhcm,bmhd->bchd', p_meta.astype(v.dtype), vm))
        return out

    y_grid = lax.map(row_block, jnp.arange(rows))
    y_grid = jnp.moveaxis(y_grid, 0, 1).reshape(B, T, H, Dh)

    k0 = kg[:, :kh, :KW]
    v0 = vg[:, :kh, :KW]
    bias0 = rpb[:, (MAX_KH - 1) + jnp.arange(kh)][:, :, (KW - 1) + jnp.arange(KW)]
    s_loc0 = jnp.einsum('bmhd,bkwhd->bhmkw', qm, k0).astype(f32) * scale + bias0[None, :, None]
    s_meta0 = jnp.einsum('bmhd,bnhd->bhmn', qm, km).astype(f32) * scale + meta_b
    p_loc0, p_meta0 = _softmax_split(s_loc0.reshape(B, H, N_META, kh * KW), s_meta0)
    p_loc0 = p_loc0.reshape(B, H, N_META, kh, KW).astype(v.dtype)
    y_meta = (jnp.einsum('bhmkw,bkwhd->bmhd', p_loc0, v0)
              + jnp.einsum('bhmn,bnhd->bmhd', p_meta0.astype(v.dtype), vm))
    return jnp.concatenate([y_meta, y_grid], axis=1)


def conv_gated_ffn(h, w_up, b_up, conv_w, conv_b, w_down):
    z = h @ w_up + b_up
    zp = jnp.pad(z, ((0, 0), (1, 1), (0, 0)))
    z = zp[:, :-2] * conv_w[0] + zp[:, 1:-1] * conv_w[1] + zp[:, 2:] * conv_w[2] + conv_b
    a, g = jnp.split(z, 2, axis=-1)
    return (a * jax.nn.gelu(g, approximate=False)) @ w_down


def encode(x, meta_tokens, ln_in_g, ln_in_b, w_in, w_pool, pool_scale, rpb, meta_bias,
           w_out, ln1_g, ln1_b, w_up, b_up, conv_w, conv_b, w_down, ln2_g, ln2_b):
    B, T, D = x.shape
    meta = jnp.broadcast_to(meta_tokens[None].astype(x.dtype), (B, N_META, D))
    h = layernorm(jnp.concatenate([meta, x], axis=1), ln_in_g, ln_in_b)
    L = h.shape[1]
    for l in range(DEPTH):
        proj = h @ w_in[l]
        u = proj[..., :D_POOL]
        q = proj[..., D_POOL:D_POOL + D_ATTN].reshape(B, L, N_HEADS, HEAD_DIM)
        k = proj[..., D_POOL + D_ATTN:D_POOL + 2 * D_ATTN].reshape(B, L, N_HEADS, HEAD_DIM)
        v = proj[..., D_POOL + 2 * D_ATTN:].reshape(B, L, N_HEADS, HEAD_DIM)
        y_pool = multi_scale_pool(u, w_pool[l], pool_scale[l])
        y_attn = neighbourhood_attention(q, k, v, rpb[l], meta_bias[l]).reshape(B, L, D_ATTN)
        mix = jnp.concatenate([y_pool, y_attn], axis=-1) @ w_out[l]
        h = layernorm(DEEPNORM_ALPHA * h + mix, ln1_g[l], ln1_b[l])
        ffn = conv_gated_ffn(h, w_up[l], b_up[l], conv_w[l], conv_b[l], w_down[l])
        h = layernorm(DEEPNORM_ALPHA * h + ffn, ln2_g[l], ln2_b[l])
    return h[:, N_META:]


def setup_inputs(seed: int = 0) -> dict:
    key = jax.random.key(seed)
    ks = jax.random.split(key, 20)
    f32 = jnp.float32

    def nrm(k, shape, s):
        return jax.random.normal(k, shape, f32) * s

    n_in = D_POOL + 3 * D_ATTN
    col_scale = jnp.concatenate([jnp.ones((D_POOL + 2 * D_ATTN,), f32),
                                 jnp.full((D_ATTN,), DEEPNORM_BETA, f32)])
    return {
        "x_prompt": nrm(ks[0], (BATCH, SEQ, D_MODEL), 1.0),
        "x_sample": nrm(ks[1], (DEC_BATCH, DEC_SEQ, D_MODEL), 1.0),
        "meta_tokens": nrm(ks[2], (N_META, D_MODEL), 1.0),
        "ln_in_g": 1.0 + nrm(ks[3], (D_MODEL,), 0.02),
        "ln_in_b": nrm(ks[4], (D_MODEL,), 0.02),
        "w_in": nrm(ks[5], (DEPTH, D_MODEL, n_in), D_MODEL ** -0.5) * col_scale,
        "w_pool": nrm(ks[6], (DEPTH, N_POOL_GROUPS, POOL_GROUP_DIM, POOL_GROUP_DIM), POOL_GROUP_DIM ** -0.5),
        "pool_scale": 1.0 + nrm(ks[7], (DEPTH, D_POOL), 0.02),
        "rpb": nrm(ks[8], (DEPTH, N_HEADS, RPB_ROWS, RPB_COLS), 0.1),
        "meta_bias": nrm(ks[9], (DEPTH, N_HEADS, N_META), 0.1),
        "w_out": nrm(ks[10], (DEPTH, D_POOL + D_ATTN, D_MODEL), (D_POOL + D_ATTN) ** -0.5 * DEEPNORM_BETA),
        "ln1_g": 1.0 + nrm(ks[11], (DEPTH, D_MODEL), 0.02),
        "ln1_b": nrm(ks[12], (DEPTH, D_MODEL), 0.02),
        "w_up": nrm(ks[13], (DEPTH, D_MODEL, 2 * D_FF), D_MODEL ** -0.5),
        "b_up": nrm(ks[14], (DEPTH, 2 * D_FF), 0.01),
        "conv_w": nrm(ks[15], (DEPTH, CONV_W, 2 * D_FF), CONV_W ** -0.5),
        "conv_b": nrm(ks[16], (DEPTH, 2 * D_FF), 0.01),
        "w_down": nrm(ks[17], (DEPTH, D_FF, D_MODEL), D_FF ** -0.5 * DEEPNORM_BETA),
        "ln2_g": 1.0 + nrm(ks[18], (DEPTH, D_MODEL), 0.02),
        "ln2_b": nrm(ks[19], (DEPTH, D_MODEL), 0.02),
    }


def reference(x_prompt, x_sample, meta_tokens, ln_in_g, ln_in_b, w_in, w_pool, pool_scale, rpb,
              meta_bias, w_out, ln1_g, ln1_b, w_up, b_up, conv_w, conv_b, w_down, ln2_g, ln2_b):
    y_prompt = encode(x_prompt, meta_tokens, ln_in_g, ln_in_b, w_in, w_pool, pool_scale, rpb,
                      meta_bias, w_out, ln1_g, ln1_b, w_up, b_up, conv_w, conv_b, w_down, ln2_g, ln2_b)
    y_sample = encode(x_sample, meta_tokens, ln_in_g, ln_in_b, w_in, w_pool, pool_scale, rpb,
                      meta_bias, w_out, ln1_g, ln1_b, w_up, b_up, conv_w, conv_b, w_down, ln2_g, ln2_b)
    return (y_prompt, y_sample)
```

```python
import functools

import numpy as np
import jax
import jax.numpy as jnp
from jax import lax
from jax.experimental import pallas as pl
from jax.experimental.pallas import tpu as pltpu

F32 = jnp.float32
BF16 = jnp.bfloat16

D_MODEL = 2048
N_META = 16
GRID_W = 64
KH = 8
KW = 16
D_POOL = D_MODEL // 2
D_ATTN = D_MODEL - D_POOL
POOL_WINDOWS = (2, 4, 8, 16)
POOL_GROUP_DIM = D_POOL // len(POOL_WINDOWS)
HEAD_DIM = 64
N_HEADS = D_ATTN // HEAD_DIM
D_FF = 5504
LN_EPS = 1e-5
DEPTH = 1
ALPHA = float((2 * DEPTH) ** 0.25)
QK_SCALE = HEAD_DIM ** -0.5
MASK_VALUE = -1e30

LANES = 128
SUBLANES = 8
PAIR = 2 * HEAD_DIM
HEAD_GROUP = 256
N_HEAD_GROUPS = D_ATTN // HEAD_GROUP
PAIRS_PER_GROUP = HEAD_GROUP // PAIR
HALO = SUBLANES
VMEM_LIMIT_BYTES = 56 * 1024 * 1024

TOKEN_TILE = 512
ROWS_PER_STEP = 8
FF_CHUNK = 512
D_FF_PAD = -(-D_FF // FF_CHUNK) * FF_CHUNK
N_FF_CHUNKS = D_FF_PAD // FF_CHUNK


def _layernorm(x, g, b):
    mu = jnp.mean(x, axis=-1, keepdims=True)
    xc = x - mu
    var = jnp.mean(xc * xc, axis=-1, keepdims=True)
    return xc * lax.rsqrt(var + LN_EPS) * g + b


def _const_spec(shape):
    zeros = (0,) * len(shape)
    return pl.BlockSpec(shape, lambda *_: zeros, pipeline_mode=pl.Buffered(1))


def _params(n_axes):
    return pltpu.CompilerParams(
        dimension_semantics=("arbitrary",) * n_axes,
        vmem_limit_bytes=VMEM_LIMIT_BYTES,
    )


def _inproj_body(x_ref, g_ref, b_ref, w_ref, h_ref, u_ref, q_ref, k_ref, v_ref):
    h = _layernorm(x_ref[0], g_ref[...], b_ref[...])
    h_ref[0] = h
    hb = h.astype(BF16)
    u_ref[0] = jnp.dot(hb, w_ref[:, 0:D_POOL], preferred_element_type=F32)
    for j in range(N_HEAD_GROUPS):
        lo = D_POOL + j * HEAD_GROUP
        q = jnp.dot(hb, w_ref[:, lo:lo + HEAD_GROUP], preferred_element_type=F32)
        q_ref[0, j] = (q * QK_SCALE).astype(BF16)
        lo += D_ATTN
        k_ref[0, j] = jnp.dot(hb, w_ref[:, lo:lo + HEAD_GROUP],
                              preferred_element_type=F32).astype(BF16)
        lo += D_ATTN
        v_ref[0, j] = jnp.dot(hb, w_ref[:, lo:lo + HEAD_GROUP],
                              preferred_element_type=F32).astype(BF16)


def _inproj(x, ln_g, ln_b, w_in_bf16, tile):
    B, T, D = x.shape
    n_in = w_in_bf16.shape[1]
    hg_shape = (B, N_HEAD_GROUPS, T, HEAD_GROUP)
    hg_spec = pl.BlockSpec((1, N_HEAD_GROUPS, tile, HEAD_GROUP), lambda b, i: (b, 0, i, 0))
    return pl.pallas_call(
        _inproj_body,
        grid=(B, T // tile),
        in_specs=[
            pl.BlockSpec((1, tile, D), lambda b, i: (b, i, 0)),
            _const_spec((1, D)),
            _const_spec((1, D)),
            _const_spec((D, n_in)),
        ],
        out_specs=[
            pl.BlockSpec((1, tile, D), lambda b, i: (b, i, 0)),
            pl.BlockSpec((1, tile, D_POOL), lambda b, i: (b, i, 0)),
            hg_spec, hg_spec, hg_spec,
        ],
        out_shape=[
            jax.ShapeDtypeStruct((B, T, D), F32),
            jax.ShapeDtypeStruct((B, T, D_POOL), F32),
            jax.ShapeDtypeStruct(hg_shape, BF16),
            jax.ShapeDtypeStruct(hg_shape, BF16),
            jax.ShapeDtypeStruct(hg_shape, BF16),
        ],
        compiler_params=_params(2),
        name="inproj",
    )(x, ln_g, ln_b, w_in_bf16)


def _attend(q2, k_win, v_win, k_meta, v_meta, bias, meta_bias):
    m = q2.shape[0]
    first = lax.broadcasted_iota(jnp.int32, (m, PAIR), 1) < HEAD_DIM
    zero = jnp.zeros_like(q2)
    qs = jnp.concatenate([jnp.where(first, q2, zero), jnp.where(first, zero, q2)], axis=0)
    contract_last = (((1,), (1,)), ((), ()))
    s = lax.dot_general(qs, k_win, contract_last, preferred_element_type=F32) + bias
    sm = lax.dot_general(qs, k_meta, contract_last, preferred_element_type=F32) + meta_bias
    mx = jnp.maximum(jnp.max(s, axis=-1, keepdims=True), jnp.max(sm, axis=-1, keepdims=True))
    e = jnp.exp(s - mx)
    em = jnp.exp(sm - mx)
    denom = jnp.sum(e, axis=-1, keepdims=True) + jnp.sum(em, axis=-1, keepdims=True)
    o = (jnp.dot(e.astype(BF16), v_win, preferred_element_type=F32)
         + jnp.dot(em.astype(BF16), v_meta, preferred_element_type=F32))
    o = o / denom
    return jnp.where(first, o[:m], o[m:])


def _attn_body(q_ref, k_ref, v_ref, qm_ref, km_ref, vm_ref, bias_ref, mb_ref,
               y_ref, ym_ref, *, rows):
    rb = pl.program_id(2)
    win = KH * GRID_W

    def one_row(ri, carry):
        r = rb * ROWS_PER_STEP + ri
        rs = jnp.clip(r - KH // 2, 0, rows - KH)
        d = r - rs
        k_off = pl.multiple_of(rs * GRID_W, GRID_W)
        q_off = pl.multiple_of(ri * GRID_W, GRID_W)
        for p in range(PAIRS_PER_GROUP):
            ls = slice(p * PAIR, (p + 1) * PAIR)
            out = _attend(
                q_ref[0, 0, pl.ds(q_off, GRID_W), ls],
                k_ref[0, 0, pl.ds(k_off, win), ls],
                v_ref[0, 0, pl.ds(k_off, win), ls],
                km_ref[0, 0, :, ls], vm_ref[0, 0, :, ls],
                bias_ref[d, p], mb_ref[0, p])
            y_ref[0, pl.ds(q_off, GRID_W), ls] = out.astype(BF16)
        return carry

    lax.fori_loop(0, ROWS_PER_STEP, one_row, 0)

    @pl.when(rb == 0)
    def _():
        for p in range(PAIRS_PER_GROUP):
            ls = slice(p * PAIR, (p + 1) * PAIR)
            bias = jnp.concatenate(
                [jnp.broadcast_to(bias_ref[0, p, 0:1, :], (N_META, win)),
                 jnp.broadcast_to(bias_ref[0, p, GRID_W:GRID_W + 1, :], (N_META, win))], axis=0)
            mb = jnp.concatenate(
                [mb_ref[0, p, 0:N_META, :], mb_ref[0, p, GRID_W:GRID_W + N_META, :]], axis=0)
            out = _attend(
                qm_ref[0, 0, :, ls], k_ref[0, 0, 0:win, ls], v_ref[0, 0, 0:win, ls],
                km_ref[0, 0, :, ls], vm_ref[0, 0, :, ls], bias, mb)
            ym_ref[0, :, ls] = out.astype(BF16)


def _attention(q, k, v, qm, km, vm, bias_tab, mb_tab):
    B, _, T, _ = q.shape
    rows = T // GRID_W
    win = KH * GRID_W
    q_rows = ROWS_PER_STEP * GRID_W
    meta_spec = pl.BlockSpec((1, 1, N_META, HEAD_GROUP), lambda b, g, i: (0, g, 0, 0))
    kv_spec = pl.BlockSpec((1, 1, T, HEAD_GROUP), lambda b, g, i: (b, g, 0, 0))
    return pl.pallas_call(
        functools.partial(_attn_body, rows=rows),
        grid=(B, N_HEAD_GROUPS, rows // ROWS_PER_STEP),
        in_specs=[
            pl.BlockSpec((1, 1, q_rows, HEAD_GROUP), lambda b, g, i: (b, g, i, 0)),
            kv_spec, kv_spec,
            meta_spec, meta_spec, meta_spec,
            pl.BlockSpec((KH, PAIRS_PER_GROUP, 2 * GRID_W, win), lambda b, g, i: (0, g, 0, 0)),
            pl.BlockSpec((1, PAIRS_PER_GROUP, 2 * GRID_W, N_META), lambda b, g, i: (g, 0, 0, 0)),
        ],
        out_specs=[
            pl.BlockSpec((1, q_rows, HEAD_GROUP), lambda b, g, i: (b, i, g)),
            pl.BlockSpec((1, N_META, HEAD_GROUP), lambda b, g, i: (b, 0, g)),
        ],
        out_shape=[
            jax.ShapeDtypeStruct((B, T, D_ATTN), BF16),
            jax.ShapeDtypeStruct((B, N_META, D_ATTN), BF16),
        ],
        compiler_params=_params(3),
        name="attention",
    )(q, k, v, qm, km, vm, bias_tab, mb_tab)


def _bias_tables(rpb, meta_bias):
    d = np.arange(KH)
    kr = np.arange(KH)
    c = np.arange(GRID_W)
    kc = np.arange(GRID_W)
    cs = np.clip(c - KW // 2, 0, GRID_W - KW)
    valid = (kc[None, :] >= cs[:, None]) & (kc[None, :] < cs[:, None] + KW)
    dc = np.clip(kc[None, :] - c[:, None] + (KW - 1), 0, 2 * KW - 2)
    dr = kr[None, :] - d[:, None] + (KH - 1)
    tab = rpb.astype(F32)[:, dr[:, :, None, None], dc[None, None, :, :]]
    tab = jnp.where(valid[None, None, None], tab, MASK_VALUE)
    tab = jnp.transpose(tab, (1, 0, 3, 2, 4))
    tab = tab.reshape(KH, N_HEADS // 2, 2 * GRID_W, KH * GRID_W)
    mb = jnp.broadcast_to(meta_bias.astype(F32)[:, None, :], (N_HEADS, GRID_W, N_META))
    mb = mb.reshape(N_HEAD_GROUPS, PAIRS_PER_GROUP, 2 * GRID_W, N_META)
    return tab, mb


def _pool_mixer(ext_ref, n, p0, seq_len, u, wp_ref, ps_ref):
    p = p0 + lax.broadcasted_iota(jnp.int32, (n, 1), 0)
    outs = []
    for g, w in enumerate(POOL_WINDOWS):
        cs = slice(g * POOL_GROUP_DIM, (g + 1) * POOL_GROUP_DIM)
        start = HALO - w // 2
        s = ext_ref[pl.ds(start, n), cs]
        for j in range(1, w):
            s = s + ext_ref[pl.ds(start + j, n), cs]
        cnt = (jnp.minimum(p + w // 2, seq_len) - jnp.maximum(p - w // 2, 0)).astype(F32)
        m = s / cnt - u[:, cs]
        y = jnp.dot(m.astype(BF16), wp_ref[g], preferred_element_type=F32) * ps_ref[:, cs]
        outs.append(y.astype(BF16))
    return jnp.concatenate(outs, axis=1)


def _mix_body(u_ref, up_ref, un_ref, um_ref, ya_ref, yam_ref, h_ref, hm_ref,
              wp_ref, ps_ref, wo_ref, g_ref, b_ref, h1_ref, h1m_ref, ext_ref, extm_ref,
              *, tile, seq_len):
    i = pl.program_id(1)
    last = pl.num_programs(1) - 1
    u = u_ref[0]
    um = um_ref[0]
    prev = jnp.where(i == 0, um[N_META - HALO:], up_ref[0])
    nxt = jnp.where(i == last, jnp.zeros((HALO, D_POOL), F32), un_ref[0])
    ext_ref[...] = jnp.concatenate([prev, u, nxt], axis=0)
    y_pool = _pool_mixer(ext_ref, tile, N_META + i * tile, seq_len, u, wp_ref, ps_ref)
    cat = jnp.concatenate([y_pool, ya_ref[0]], axis=1)
    mix = jnp.dot(cat, wo_ref[...], preferred_element_type=F32)
    h1_ref[0] = _layernorm(ALPHA * h_ref[0] + mix, g_ref[...], b_ref[...])

    @pl.when(i == 0)
    def _():
        extm_ref[...] = jnp.concatenate(
            [jnp.zeros((HALO, D_POOL), F32), um, u[:HALO]], axis=0)
        y_pool_m = _pool_mixer(extm_ref, N_META, 0, seq_len, um, wp_ref, ps_ref)
        cat_m = jnp.concatenate([y_pool_m, yam_ref[0]], axis=1)
        mix_m = jnp.dot(cat_m, wo_ref[...], preferred_element_type=F32)
        h1m_ref[0] = _layernorm(ALPHA * hm_ref[0] + mix_m, g_ref[...], b_ref[...])


def _halo_specs(tile, n_tokens, width):
    per_tile = tile // HALO
    n_blocks = n_tokens // HALO
    prev = pl.BlockSpec((1, HALO, width),
                        lambda b, i, *_: (b, jnp.maximum(i * per_tile - 1, 0), 0))
    nxt = pl.BlockSpec((1, HALO, width),
                       lambda b, i, *_: (b, jnp.minimum((i + 1) * per_tile, n_blocks - 1), 0))
    return prev, nxt


def _mix(u, um, ya, yam, h, hm, w_pool_bf16, pool_scale, w_out_bf16, ln_g, ln_b, tile):
    B, T, D = h.shape
    prev_spec, next_spec = _halo_specs(tile, T, D_POOL)
    meta = lambda width: pl.BlockSpec((1, N_META, width), lambda b, i: (0, 0, 0))
    meta_b = lambda width: pl.BlockSpec((1, N_META, width), lambda b, i: (b, 0, 0))
    return pl.pallas_call(
        functools.partial(_mix_body, tile=tile, seq_len=N_META + T),
        grid=(B, T // tile),
        in_specs=[
            pl.BlockSpec((1, tile, D_POOL), lambda b, i: (b, i, 0)),
            prev_spec, next_spec,
            meta(D_POOL),
            pl.BlockSpec((1, tile, D_ATTN), lambda b, i: (b, i, 0)),
            meta_b(D_ATTN),
            pl.BlockSpec((1, tile, D), lambda b, i: (b, i, 0)),
            meta(D),
            _const_spec(w_pool_bf16.shape),
            _const_spec((1, D_POOL)),
            _const_spec((D, D)),
            _const_spec((1, D)),
            _const_spec((1, D)),
        ],
        out_specs=[
            pl.BlockSpec((1, tile, D), lambda b, i: (b, i, 0)),
            meta_b(D),
        ],
        out_shape=[
            jax.ShapeDtypeStruct((B, T, D), F32),
            jax.ShapeDtypeStruct((B, N_META, D), F32),
        ],
        scratch_shapes=[
            pltpu.VMEM((tile + 2 * HALO, D_POOL), F32),
            pltpu.VMEM((N_META + 2 * HALO, D_POOL), F32),
        ],
        compiler_params=_params(2),
        name="mix",
    )(u, u, u, um, ya, yam, h, hm, w_pool_bf16, pool_scale, w_out_bf16, ln_g, ln_b)


def _ffn_body(h1_ref, hp_ref, hn_ref, h1m_ref, wa_ref, wg_ref, ba_ref, bg_ref,
              cwa_ref, cwg_ref, cba_ref, cbg_ref, wd_ref, g_ref, b_ref, o_ref,
              lhs_ref, za_ref, zg_ref, acc_ref, *, tile):
    i = pl.program_id(1)
    c = pl.program_id(2)
    last_tile = pl.num_programs(1) - 1
    last_chunk = pl.num_programs(2) - 1

    @pl.when(c == 0)
    def _():
        prev = jnp.where(i == 0, h1m_ref[0, N_META - HALO:, :], hp_ref[0])
        lhs_ref[...] = jnp.concatenate([prev, h1_ref[0], hn_ref[0]], axis=0).astype(BF16)
        acc_ref[...] = jnp.zeros_like(acc_ref)

    lhs = lhs_ref[...]
    za_ref[...] = jnp.dot(lhs, wa_ref[...], preferred_element_type=F32) + ba_ref[...]
    zg_ref[...] = jnp.dot(lhs, wg_ref[...], preferred_element_type=F32) + bg_ref[...]

    @pl.when(i == last_tile)
    def _():
        za_ref[HALO + tile:, :] = jnp.zeros((HALO, FF_CHUNK), F32)
        zg_ref[HALO + tile:, :] = jnp.zeros((HALO, FF_CHUNK), F32)

    def conv(z_ref, cw_ref, cb_ref):
        return (z_ref[pl.ds(HALO - 1, tile), :] * cw_ref[0:1, :]
                + z_ref[pl.ds(HALO, tile), :] * cw_ref[1:2, :]
                + z_ref[pl.ds(HALO + 1, tile), :] * cw_ref[2:3, :]
                + cb_ref[...])

    a = conv(za_ref, cwa_ref, cba_ref)
    g = conv(zg_ref, cwg_ref, cbg_ref)
    sqrt_half = np.sqrt(0.5).astype(np.float32)
    act = a * (0.5 * g * (1.0 + lax.erf(g * sqrt_half)))
    acc_ref[...] += jnp.dot(act.astype(BF16), wd_ref[...], preferred_element_type=F32)

    @pl.when(c == last_chunk)
    def _():
        o_ref[0] = _layernorm(ALPHA * h1_ref[0] + acc_ref[...], g_ref[...], b_ref[...])


def _ffn(h1, h1m, w_up_p, b_up_p, conv_w_p, conv_b_p, w_down_p, ln_g, ln_b, tile):
    B, T, D = h1.shape
    prev_spec, next_spec = _halo_specs(tile, T, D)
    col_a = lambda rows: pl.BlockSpec((rows, FF_CHUNK), lambda b, i, c: (0, c))
    col_g = lambda rows: pl.BlockSpec((rows, FF_CHUNK), lambda b, i, c: (0, N_FF_CHUNKS + c))
    const3 = lambda shape: pl.BlockSpec(shape, lambda b, i, c: (0,) * len(shape),
                                        pipeline_mode=pl.Buffered(1))
    return pl.pallas_call(
        functools.partial(_ffn_body, tile=tile),
        grid=(B, T // tile, N_FF_CHUNKS),
        in_specs=[
            pl.BlockSpec((1, tile, D), lambda b, i, c: (b, i, 0)),
            prev_spec, next_spec,
            pl.BlockSpec((1, N_META, D), lambda b, i, c: (b, 0, 0)),
            col_a(D), col_g(D),
            col_a(1), col_g(1),
            col_a(3), col_g(3),
            col_a(1), col_g(1),
            pl.BlockSpec((FF_CHUNK, D), lambda b, i, c: (c, 0)),
            const3((1, D)), const3((1, D)),
        ],
        out_specs=pl.BlockSpec((1, tile, D), lambda b, i, c: (b, i, 0)),
        out_shape=jax.ShapeDtypeStruct((B, T, D), F32),
        scratch_shapes=[
            pltpu.VMEM((tile + 2 * HALO, D), BF16),
            pltpu.VMEM((tile + 2 * HALO, FF_CHUNK), F32),
            pltpu.VMEM((tile + 2 * HALO, FF_CHUNK), F32),
            pltpu.VMEM((tile, D), F32),
        ],
        compiler_params=_params(3),
        name="ffn",
    )(h1, h1, h1, h1m, w_up_p, w_up_p, b_up_p, b_up_p, conv_w_p, conv_w_p,
      conv_b_p, conv_b_p, w_down_p, ln_g, ln_b)


def _pad_ff_cols(a):
    pad = [(0, 0)] * (a.ndim - 1) + [(0, D_FF_PAD - D_FF)]
    return jnp.concatenate([jnp.pad(a[..., :D_FF], pad), jnp.pad(a[..., D_FF:], pad)], axis=-1)


def _encode(x, meta, consts):
    (ln_in_g, ln_in_b, w_in, w_pool, pool_scale, bias_tab, mb_tab, w_out, ln1_g, ln1_b,
     w_up, b_up, conv_w, conv_b, w_down, ln2_g, ln2_b) = consts
    hm, um, qm, km, vm = meta
    h, u, q, k, v = _inproj(x, ln_in_g, ln_in_b, w_in, TOKEN_TILE)
    ya, yam = _attention(q, k, v, qm, km, vm, bias_tab, mb_tab)
    h1, h1m = _mix(u, um, ya, yam, h, hm, w_pool, pool_scale, w_out, ln1_g, ln1_b, TOKEN_TILE)
    return _ffn(h1, h1m, w_up, b_up, conv_w, conv_b, w_down, ln2_g, ln2_b, TOKEN_TILE)


def kernel(x_prompt, x_sample, meta_tokens, ln_in_g, ln_in_b, w_in, w_pool, pool_scale, rpb,
           meta_bias, w_out, ln1_g, ln1_b, w_up, b_up, conv_w, conv_b, w_down, ln2_g, ln2_b):
    row = lambda a: a.reshape(1, -1).astype(F32)
    bias_tab, mb_tab = _bias_tables(rpb[0], meta_bias[0])
    w_down_p = jnp.pad(w_down[0].astype(BF16), ((0, D_FF_PAD - D_FF), (0, 0)))
    consts = (
        row(ln_in_g), row(ln_in_b), w_in[0].astype(BF16), w_pool[0].astype(BF16),
        row(pool_scale[0]), bias_tab, mb_tab, w_out[0].astype(BF16), row(ln1_g[0]), row(ln1_b[0]),
        _pad_ff_cols(w_up[0].astype(BF16)), _pad_ff_cols(row(b_up[0])),
        _pad_ff_cols(conv_w[0].astype(F32)), _pad_ff_cols(row(conv_b[0])), w_down_p,
        row(ln2_g[0]), row(ln2_b[0]),
    )
    meta = _inproj(meta_tokens[None].astype(F32), consts[0], consts[1], consts[2], N_META)
    y_prompt = _encode(x_prompt, meta, consts)
    y_sample = _encode(x_sample, meta, consts)
    return (y_prompt, y_sample)
```

```python
import functools

import numpy as np
import jax
import jax.numpy as jnp
from jax import lax
from jax.experimental import pallas as pl
from jax.experimental.pallas import tpu as pltpu

F32 = jnp.float32
BF16 = jnp.bfloat16

D_MODEL = 2048
N_META = 16
GRID_W = 64
KH = 8
KW = 16
D_POOL = D_MODEL // 2
D_ATTN = D_MODEL - D_POOL
POOL_WINDOWS = (2, 4, 8, 16)
POOL_GROUP_DIM = D_POOL // len(POOL_WINDOWS)
HEAD_DIM = 64
N_HEADS = D_ATTN // HEAD_DIM
D_FF = 5504
LN_EPS = 1e-5
DEPTH = 1
ALPHA = float((2 * DEPTH) ** 0.25)
QK_SCALE = HEAD_DIM ** -0.5
MASK_VALUE = -1e30

LANES = 128
SUBLANES = 8
PAIR = 2 * HEAD_DIM
HEAD_GROUP = 256
N_HEAD_GROUPS = D_ATTN // HEAD_GROUP
PAIRS_PER_GROUP = HEAD_GROUP // PAIR
HALO = SUBLANES
VMEM_LIMIT_BYTES = 56 * 1024 * 1024

TOKEN_TILE = 512
ROWS_PER_STEP = 8
FF_CHUNK = 512
D_FF_PAD = -(-D_FF // FF_CHUNK) * FF_CHUNK
N_FF_CHUNKS = D_FF_PAD // FF_CHUNK
FF_SUB = 256


def _layernorm(x, g, b):
    mu = jnp.mean(x, axis=-1, keepdims=True)
    xc = x - mu
    var = jnp.mean(xc * xc, axis=-1, keepdims=True)
    return xc * lax.rsqrt(var + LN_EPS) * g + b


def _const_spec(shape):
    zeros = (0,) * len(shape)
    return pl.BlockSpec(shape, lambda *_: zeros, pipeline_mode=pl.Buffered(1))


def _params(n_axes):
    return pltpu.CompilerParams(
        dimension_semantics=("arbitrary",) * n_axes,
        vmem_limit_bytes=VMEM_LIMIT_BYTES,
    )


def _inproj_body(x_ref, g_ref, b_ref, w_ref, h_ref, u_ref, q_ref, k_ref, v_ref):
    h = _layernorm(x_ref[0], g_ref[...], b_ref[...])
    h_ref[0] = h
    hb = h.astype(BF16)
    u_ref[0] = jnp.dot(hb, w_ref[:, 0:D_POOL], preferred_element_type=F32)
    for j in range(N_HEAD_GROUPS):
        lo = D_POOL + j * HEAD_GROUP
        q = jnp.dot(hb, w_ref[:, lo:lo + HEAD_GROUP], preferred_element_type=F32)
        q_ref[0, j] = (q * QK_SCALE).astype(BF16)
        lo += D_ATTN
        k_ref[0, j] = jnp.dot(hb, w_ref[:, lo:lo + HEAD_GROUP],
                              preferred_element_type=F32).astype(BF16)
        lo += D_ATTN
        v_ref[0, j] = jnp.dot(hb, w_ref[:, lo:lo + HEAD_GROUP],
                              preferred_element_type=F32).astype(BF16)


def _inproj(x, ln_g, ln_b, w_in_bf16, tile):
    B, T, D = x.shape
    n_in = w_in_bf16.shape[1]
    hg_shape = (B, N_HEAD_GROUPS, T, HEAD_GROUP)
    hg_spec = pl.BlockSpec((1, N_HEAD_GROUPS, tile, HEAD_GROUP), lambda b, i: (b, 0, i, 0))
    return pl.pallas_call(
        _inproj_body,
        grid=(B, T // tile),
        in_specs=[
            pl.BlockSpec((1, tile, D), lambda b, i: (b, i, 0)),
            _const_spec((1, D)),
            _const_spec((1, D)),
            _const_spec((D, n_in)),
        ],
        out_specs=[
            pl.BlockSpec((1, tile, D), lambda b, i: (b, i, 0)),
            pl.BlockSpec((1, tile, D_POOL), lambda b, i: (b, i, 0)),
            hg_spec, hg_spec, hg_spec,
        ],
        out_shape=[
            jax.ShapeDtypeStruct((B, T, D), F32),
            jax.ShapeDtypeStruct((B, T, D_POOL), F32),
            jax.ShapeDtypeStruct(hg_shape, BF16),
            jax.ShapeDtypeStruct(hg_shape, BF16),
            jax.ShapeDtypeStruct(hg_shape, BF16),
        ],
        compiler_params=_params(2),
        name="inproj",
    )(x, ln_g, ln_b, w_in_bf16)


def _attend(q2, k_win, v_win, k_meta, v_meta, bias, meta_bias):
    m = q2.shape[0]
    first = lax.broadcasted_iota(jnp.int32, (m, PAIR), 1) < HEAD_DIM
    zero = jnp.zeros_like(q2)
    qs = jnp.concatenate([jnp.where(first, q2, zero), jnp.where(first, zero, q2)], axis=0)
    contract_last = (((1,), (1,)), ((), ()))
    s = lax.dot_general(qs, k_win, contract_last, preferred_element_type=F32) + bias
    sm = lax.dot_general(qs, k_meta, contract_last, preferred_element_type=F32) + meta_bias
    mx = jnp.maximum(jnp.max(s, axis=-1, keepdims=True), jnp.max(sm, axis=-1, keepdims=True))
    e = jnp.exp(s - mx)
    em = jnp.exp(sm - mx)
    denom = jnp.sum(e, axis=-1, keepdims=True) + jnp.sum(em, axis=-1, keepdims=True)
    o = (jnp.dot(e.astype(BF16), v_win, preferred_element_type=F32)
         + jnp.dot(em.astype(BF16), v_meta, preferred_element_type=F32))
    o = o / denom
    return jnp.where(first, o[:m], o[m:])


def _attn_body(q_ref, k_ref, v_ref, qm_ref, km_ref, vm_ref, bias_ref, mb_ref,
               y_ref, ym_ref, *, rows):
    rb = pl.program_id(2)
    win = KH * GRID_W

    def one_row(ri, carry):
        r = rb * ROWS_PER_STEP + ri
        rs = jnp.clip(r - KH // 2, 0, rows - KH)
        d = r - rs
        k_off = pl.multiple_of(rs * GRID_W, GRID_W)
        q_off = pl.multiple_of(ri * GRID_W, GRID_W)
        for p in range(PAIRS_PER_GROUP):
            ls = slice(p * PAIR, (p + 1) * PAIR)
            out = _attend(
                q_ref[0, 0, pl.ds(q_off, GRID_W), ls],
                k_ref[0, 0, pl.ds(k_off, win), ls],
                v_ref[0, 0, pl.ds(k_off, win), ls],
                km_ref[0, 0, :, ls], vm_ref[0, 0, :, ls],
                bias_ref[d, p], mb_ref[0, p])
            y_ref[0, pl.ds(q_off, GRID_W), ls] = out.astype(BF16)
        return carry

    lax.fori_loop(0, ROWS_PER_STEP, one_row, 0, unroll=True)

    @pl.when(rb == 0)
    def _():
        for p in range(PAIRS_PER_GROUP):
            ls = slice(p * PAIR, (p + 1) * PAIR)
            bias = jnp.concatenate(
                [jnp.broadcast_to(bias_ref[0, p, 0:1, :], (N_META, win)),
                 jnp.broadcast_to(bias_ref[0, p, GRID_W:GRID_W + 1, :], (N_META, win))], axis=0)
            mb = jnp.concatenate(
                [mb_ref[0, p, 0:N_META, :], mb_ref[0, p, GRID_W:GRID_W + N_META, :]], axis=0)
            out = _attend(
                qm_ref[0, 0, :, ls], k_ref[0, 0, 0:win, ls], v_ref[0, 0, 0:win, ls],
                km_ref[0, 0, :, ls], vm_ref[0, 0, :, ls], bias, mb)
            ym_ref[0, :, ls] = out.astype(BF16)


def _attention(q, k, v, qm, km, vm, bias_tab, mb_tab):
    B, _, T, _ = q.shape
    rows = T // GRID_W
    win = KH * GRID_W
    q_rows = ROWS_PER_STEP * GRID_W
    meta_spec = pl.BlockSpec((1, 1, N_META, HEAD_GROUP), lambda b, g, i: (0, g, 0, 0))
    kv_spec = pl.BlockSpec((1, 1, T, HEAD_GROUP), lambda b, g, i: (b, g, 0, 0))
    return pl.pallas_call(
        functools.partial(_attn_body, rows=rows),
        grid=(B, N_HEAD_GROUPS, rows // ROWS_PER_STEP),
        in_specs=[
            pl.BlockSpec((1, 1, q_rows, HEAD_GROUP), lambda b, g, i: (b, g, i, 0)),
            kv_spec, kv_spec,
            meta_spec, meta_spec, meta_spec,
            pl.BlockSpec((KH, PAIRS_PER_GROUP, 2 * GRID_W, win), lambda b, g, i: (0, g, 0, 0)),
            pl.BlockSpec((1, PAIRS_PER_GROUP, 2 * GRID_W, N_META), lambda b, g, i: (g, 0, 0, 0)),
        ],
        out_specs=[
            pl.BlockSpec((1, q_rows, HEAD_GROUP), lambda b, g, i: (b, i, g)),
            pl.BlockSpec((1, N_META, HEAD_GROUP), lambda b, g, i: (b, 0, g)),
        ],
        out_shape=[
            jax.ShapeDtypeStruct((B, T, D_ATTN), BF16),
            jax.ShapeDtypeStruct((B, N_META, D_ATTN), BF16),
        ],
        compiler_params=_params(3),
        name="attention",
    )(q, k, v, qm, km, vm, bias_tab, mb_tab)


def _bias_tables(rpb, meta_bias):
    c = np.arange(GRID_W)
    kc = np.arange(GRID_W)
    cs = np.clip(c - KW // 2, 0, GRID_W - KW)
    valid = (kc[None, :] >= cs[:, None]) & (kc[None, :] < cs[:, None] + KW)
    dc = kc[None, :] - c[:, None] + (KW - 1)
    onehot = ((dc[None] == np.arange(2 * KW - 1)[:, None, None]) & valid[None]).astype(np.float32)
    toep = jnp.einsum('hrj,jck->hcrk', rpb.astype(F32), onehot,
                      precision=lax.Precision.HIGHEST)
    toep = jnp.where(valid[None, :, None, :], toep, MASK_VALUE)
    tab = jnp.stack([toep[:, :, KH - 1 - d:2 * KH - 1 - d, :].reshape(N_HEADS, GRID_W, KH * GRID_W)
                     for d in range(KH)])
    tab = tab.reshape(KH, N_HEADS // 2, 2 * GRID_W, KH * GRID_W)
    mb = jnp.broadcast_to(meta_bias.astype(F32)[:, None, :], (N_HEADS, GRID_W, N_META))
    mb = mb.reshape(N_HEAD_GROUPS, PAIRS_PER_GROUP, 2 * GRID_W, N_META)
    return tab, mb


def _pool_mixer(ext_ref, n, p0, seq_len, u, wp_ref, ps_ref):
    p = p0 + lax.broadcasted_iota(jnp.int32, (n, 1), 0)
    outs = []
    for g, w in enumerate(POOL_WINDOWS):
        cs = slice(g * POOL_GROUP_DIM, (g + 1) * POOL_GROUP_DIM)
        start = HALO - w // 2
        s = ext_ref[pl.ds(start, n), cs]
        for j in range(1, w):
            s = s + ext_ref[pl.ds(start + j, n), cs]
        cnt = (jnp.minimum(p + w // 2, seq_len) - jnp.maximum(p - w // 2, 0)).astype(F32)
        m = s / cnt - u[:, cs]
        y = jnp.dot(m.astype(BF16), wp_ref[g], preferred_element_type=F32) * ps_ref[:, cs]
        outs.append(y.astype(BF16))
    return jnp.concatenate(outs, axis=1)


def _mix_body(u_ref, up_ref, un_ref, um_ref, ya_ref, yam_ref, h_ref, hm_ref,
              wp_ref, ps_ref, wo_ref, g_ref, b_ref, h1_ref, h1m_ref, ext_ref, extm_ref,
              *, tile, seq_len):
    i = pl.program_id(1)
    last = pl.num_programs(1) - 1
    u = u_ref[0]
    um = um_ref[0]
    prev = jnp.where(i == 0, um[N_META - HALO:], up_ref[0])
    nxt = jnp.where(i == last, jnp.zeros((HALO, D_POOL), F32), un_ref[0])
    ext_ref[...] = jnp.concatenate([prev, u, nxt], axis=0)
    y_pool = _pool_mixer(ext_ref, tile, N_META + i * tile, seq_len, u, wp_ref, ps_ref)
    cat = jnp.concatenate([y_pool, ya_ref[0]], axis=1)
    mix = jnp.dot(cat, wo_ref[...], preferred_element_type=F32)
    h1_ref[0] = _layernorm(ALPHA * h_ref[0] + mix, g_ref[...], b_ref[...])

    @pl.when(i == 0)
    def _():
        extm_ref[...] = jnp.concatenate(
            [jnp.zeros((HALO, D_POOL), F32), um, u[:HALO]], axis=0)
        y_pool_m = _pool_mixer(extm_ref, N_META, 0, seq_len, um, wp_ref, ps_ref)
        cat_m = jnp.concatenate([y_pool_m, yam_ref[0]], axis=1)
        mix_m = jnp.dot(cat_m, wo_ref[...], preferred_element_type=F32)
        h1m_ref[0] = _layernorm(ALPHA * hm_ref[0] + mix_m, g_ref[...], b_ref[...])


def _halo_specs(tile, n_tokens, width):
    per_tile = tile // HALO
    n_blocks = n_tokens // HALO
    prev = pl.BlockSpec((1, HALO, width),
                        lambda b, i, *_: (b, jnp.maximum(i * per_tile - 1, 0), 0))
    nxt = pl.BlockSpec((1, HALO, width),
                       lambda b, i, *_: (b, jnp.minimum((i + 1) * per_tile, n_blocks - 1), 0))
    return prev, nxt


def _mix(u, um, ya, yam, h, hm, w_pool_bf16, pool_scale, w_out_bf16, ln_g, ln_b, tile):
    B, T, D = h.shape
    prev_spec, next_spec = _halo_specs(tile, T, D_POOL)
    meta = lambda width: pl.BlockSpec((1, N_META, width), lambda b, i: (0, 0, 0))
    meta_b = lambda width: pl.BlockSpec((1, N_META, width), lambda b, i: (b, 0, 0))
    return pl.pallas_call(
        functools.partial(_mix_body, tile=tile, seq_len=N_META + T),
        grid=(B, T // tile),
        in_specs=[
            pl.BlockSpec((1, tile, D_POOL), lambda b, i: (b, i, 0)),
            prev_spec, next_spec,
            meta(D_POOL),
            pl.BlockSpec((1, tile, D_ATTN), lambda b, i: (b, i, 0)),
            meta_b(D_ATTN),
            pl.BlockSpec((1, tile, D), lambda b, i: (b, i, 0)),
            meta(D),
            _const_spec(w_pool_bf16.shape),
            _const_spec((1, D_POOL)),
            _const_spec((D, D)),
            _const_spec((1, D)),
            _const_spec((1, D)),
        ],
        out_specs=[
            pl.BlockSpec((1, tile, D), lambda b, i: (b, i, 0)),
            meta_b(D),
        ],
        out_shape=[
            jax.ShapeDtypeStruct((B, T, D), F32),
            jax.ShapeDtypeStruct((B, N_META, D), F32),
        ],
        scratch_shapes=[
            pltpu.VMEM((tile + 2 * HALO, D_POOL), F32),
            pltpu.VMEM((N_META + 2 * HALO, D_POOL), F32),
        ],
        compiler_params=_params(2),
        name="mix",
    )(u, u, u, um, ya, yam, h, hm, w_pool_bf16, pool_scale, w_out_bf16, ln_g, ln_b)


def _ffn_body(h1_ref, hp_ref, hn_ref, h1m_ref, wa_ref, wg_ref, ba_ref, bg_ref,
              cwa_ref, cwg_ref, cba_ref, cbg_ref, wd_ref, g_ref, b_ref, o_ref,
              lhs_ref, za_ref, zg_ref, acc_ref, *, tile):
    i = pl.program_id(1)
    c = pl.program_id(2)
    last_tile = pl.num_programs(1) - 1
    last_chunk = pl.num_programs(2) - 1

    @pl.when(c == 0)
    def _():
        prev = jnp.where(i == 0, h1m_ref[0, N_META - HALO:, :], hp_ref[0])
        lhs_ref[...] = jnp.concatenate([prev, h1_ref[0], hn_ref[0]], axis=0).astype(BF16)
        acc_ref[...] = jnp.zeros_like(acc_ref)

    lhs = lhs_ref[...]
    is_seq_end = i == last_tile
    sqrt_half = np.sqrt(0.5).astype(np.float32)

    def up_conv(s, z_ref, w_ref, b_ref, cw_ref, cb_ref):
        cs = slice(s * FF_SUB, (s + 1) * FF_SUB)
        z = jnp.dot(lhs, w_ref[:, cs], preferred_element_type=F32) + b_ref[:, cs]
        z_ref[s] = z
        z_ref[s, HALO + tile:, :] = jnp.where(is_seq_end, 0.0, z[HALO + tile:, :])
        return (z_ref[s, pl.ds(HALO - 1, tile), :] * cw_ref[0:1, cs]
                + z_ref[s, pl.ds(HALO, tile), :] * cw_ref[1:2, cs]
                + z_ref[s, pl.ds(HALO + 1, tile), :] * cw_ref[2:3, cs]
                + cb_ref[:, cs])

    down = None
    for s in range(FF_CHUNK // FF_SUB):
        a = up_conv(s, za_ref, wa_ref, ba_ref, cwa_ref, cba_ref)
        g = up_conv(s, zg_ref, wg_ref, bg_ref, cwg_ref, cbg_ref)
        act = a * (0.5 * g * (1.0 + lax.erf(g * sqrt_half)))
        part = jnp.dot(act.astype(BF16), wd_ref[s * FF_SUB:(s + 1) * FF_SUB, :],
                       preferred_element_type=F32)
        down = part if down is None else down + part
    acc_ref[...] += down

    @pl.when(c == last_chunk)
    def _():
        o_ref[0] = _layernorm(ALPHA * h1_ref[0] + acc_ref[...], g_ref[...], b_ref[...])


def _ffn(h1, h1m, w_up_p, b_up_p, conv_w_p, conv_b_p, w_down_p, ln_g, ln_b, tile):
    B, T, D = h1.shape
    prev_spec, next_spec = _halo_specs(tile, T, D)
    col_a = lambda rows: pl.BlockSpec((rows, FF_CHUNK), lambda b, i, c: (0, c))
    col_g = lambda rows: pl.BlockSpec((rows, FF_CHUNK), lambda b, i, c: (0, N_FF_CHUNKS + c))
    const3 = lambda shape: pl.BlockSpec(shape, lambda b, i, c: (0,) * len(shape),
                                        pipeline_mode=pl.Buffered(1))
    return pl.pallas_call(
        functools.partial(_ffn_body, tile=tile),
        grid=(B, T // tile, N_FF_CHUNKS),
        in_specs=[
            pl.BlockSpec((1, tile, D), lambda b, i, c: (b, i, 0)),
            prev_spec, next_spec,
            pl.BlockSpec((1, N_META, D), lambda b, i, c: (b, 0, 0)),
            col_a(D), col_g(D),
            col_a(1), col_g(1),
            col_a(3), col_g(3),
            col_a(1), col_g(1),
            pl.BlockSpec((FF_CHUNK, D), lambda b, i, c: (c, 0)),
            const3((1, D)), const3((1, D)),
        ],
        out_specs=pl.BlockSpec((1, tile, D), lambda b, i, c: (b, i, 0)),
        out_shape=jax.ShapeDtypeStruct((B, T, D), F32),
        scratch_shapes=[
            pltpu.VMEM((tile + 2 * HALO, D), BF16),
            pltpu.VMEM((FF_CHUNK // FF_SUB, tile + 2 * HALO, FF_SUB), F32),
            pltpu.VMEM((FF_CHUNK // FF_SUB, tile + 2 * HALO, FF_SUB), F32),
            pltpu.VMEM((tile, D), F32),
        ],
        compiler_params=_params(3),
        name="ffn",
    )(h1, h1, h1, h1m, w_up_p, w_up_p, b_up_p, b_up_p, conv_w_p, conv_w_p,
      conv_b_p, conv_b_p, w_down_p, ln_g, ln_b)


def _pad_ff_cols(a):
    pad = [(0, 0)] * (a.ndim - 1) + [(0, D_FF_PAD - D_FF)]
    return jnp.concatenate([jnp.pad(a[..., :D_FF], pad), jnp.pad(a[..., D_FF:], pad)], axis=-1)


def _encode(x, meta, consts):
    (ln_in_g, ln_in_b, w_in, w_pool, pool_scale, bias_tab, mb_tab, w_out, ln1_g, ln1_b,
     w_up, b_up, conv_w, conv_b, w_down, ln2_g, ln2_b) = consts
    hm, um, qm, km, vm = meta
    h, u, q, k, v = _inproj(x, ln_in_g, ln_in_b, w_in, TOKEN_TILE)
    ya, yam = _attention(q, k, v, qm, km, vm, bias_tab, mb_tab)
    h1, h1m = _mix(u, um, ya, yam, h, hm, w_pool, pool_scale, w_out, ln1_g, ln1_b, TOKEN_TILE)
    return _ffn(h1, h1m, w_up, b_up, conv_w, conv_b, w_down, ln2_g, ln2_b, TOKEN_TILE)


def kernel(x_prompt, x_sample, meta_tokens, ln_in_g, ln_in_b, w_in, w_pool, pool_scale, rpb,
           meta_bias, w_out, ln1_g, ln1_b, w_up, b_up, conv_w, conv_b, w_down, ln2_g, ln2_b):
    row = lambda a: a.reshape(1, -1).astype(F32)
    bias_tab, mb_tab = _bias_tables(rpb[0], meta_bias[0])
    w_down_p = jnp.pad(w_down[0].astype(BF16), ((0, D_FF_PAD - D_FF), (0, 0)))
    consts = (
        row(ln_in_g), row(ln_in_b), w_in[0].astype(BF16), w_pool[0].astype(BF16),
        row(pool_scale[0]), bias_tab, mb_tab, w_out[0].astype(BF16), row(ln1_g[0]), row(ln1_b[0]),
        _pad_ff_cols(w_up[0].astype(BF16)), _pad_ff_cols(row(b_up[0])),
        _pad_ff_cols(conv_w[0].astype(F32)), _pad_ff_cols(row(conv_b[0])), w_down_p,
        row(ln2_g[0]), row(ln2_b[0]),
    )
    meta = _inproj(meta_tokens[None].astype(F32), consts[0], consts[1], consts[2], N_META)
    y_prompt = _encode(x_prompt, meta, consts)
    y_sample = _encode(x_sample, meta, consts)
    return (y_prompt, y_sample)
```

```python
import functools

import numpy as np
import jax
import jax.numpy as jnp
from jax import lax
from jax.experimental import pallas as pl
from jax.experimental.pallas import tpu as pltpu

F32 = jnp.float32
BF16 = jnp.bfloat16

D_MODEL = 2048
N_META = 16
GRID_W = 64
KH = 8
KW = 16
D_POOL = D_MODEL // 2
D_ATTN = D_MODEL - D_POOL
POOL_WINDOWS = (2, 4, 8, 16)
POOL_GROUP_DIM = D_POOL // len(POOL_WINDOWS)
HEAD_DIM = 64
N_HEADS = D_ATTN // HEAD_DIM
D_FF = 5504
LN_EPS = 1e-5
DEPTH = 1
ALPHA = float((2 * DEPTH) ** 0.25)
QK_SCALE = HEAD_DIM ** -0.5
MASK_VALUE = -1e30

LANES = 128
SUBLANES = 8
PAIR = 2 * HEAD_DIM
HEAD_GROUP = 256
N_HEAD_GROUPS = D_ATTN // HEAD_GROUP
PAIRS_PER_GROUP = HEAD_GROUP // PAIR
HALO = SUBLANES
VMEM_LIMIT_BYTES = 56 * 1024 * 1024

TOKEN_TILE = 512
ROWS_PER_STEP = 8
ROWS_PER_BLOCK = 4
KEY_ROWS = ROWS_PER_BLOCK + KH - 1
ATTN_KEYS = (KEY_ROWS + 1) * GRID_W
BLOCK_Q = ROWS_PER_BLOCK * GRID_W
N_VARIANTS = 3
FF_CHUNK = 512
D_FF_PAD = -(-D_FF // FF_CHUNK) * FF_CHUNK
N_FF_CHUNKS = D_FF_PAD // FF_CHUNK
FF_SUB = 256


def _layernorm(x, g, b):
    mu = jnp.mean(x, axis=-1, keepdims=True)
    xc = x - mu
    var = jnp.mean(xc * xc, axis=-1, keepdims=True)
    return xc * lax.rsqrt(var + LN_EPS) * g + b


def _const_spec(shape):
    zeros = (0,) * len(shape)
    return pl.BlockSpec(shape, lambda *_: zeros, pipeline_mode=pl.Buffered(1))


def _params(n_axes, flags=None):
    return pltpu.CompilerParams(
        dimension_semantics=("arbitrary",) * n_axes,
        vmem_limit_bytes=VMEM_LIMIT_BYTES,
        flags=flags,
    )


def _inproj_body(x_ref, g_ref, b_ref, w_ref, h_ref, u_ref, q_ref, k_ref, v_ref):
    h = _layernorm(x_ref[0], g_ref[...], b_ref[...])
    h_ref[0] = h
    hb = h.astype(BF16)
    u_ref[0] = jnp.dot(hb, w_ref[:, 0:D_POOL], preferred_element_type=F32)
    for j in range(N_HEAD_GROUPS):
        lo = D_POOL + j * HEAD_GROUP
        q = jnp.dot(hb, w_ref[:, lo:lo + HEAD_GROUP], preferred_element_type=F32)
        q_ref[0, j] = (q * QK_SCALE).astype(BF16)
        lo += D_ATTN
        k_ref[0, j] = jnp.dot(hb, w_ref[:, lo:lo + HEAD_GROUP],
                              preferred_element_type=F32).astype(BF16)
        lo += D_ATTN
        v_ref[0, j] = jnp.dot(hb, w_ref[:, lo:lo + HEAD_GROUP],
                              preferred_element_type=F32).astype(BF16)


def _inproj(x, ln_g, ln_b, w_in_bf16, tile):
    B, T, D = x.shape
    n_in = w_in_bf16.shape[1]
    hg_shape = (B, N_HEAD_GROUPS, T, HEAD_GROUP)
    hg_spec = pl.BlockSpec((1, N_HEAD_GROUPS, tile, HEAD_GROUP), lambda b, i: (b, 0, i, 0))
    return pl.pallas_call(
        _inproj_body,
        grid=(B, T // tile),
        in_specs=[
            pl.BlockSpec((1, tile, D), lambda b, i: (b, i, 0)),
            _const_spec((1, D)),
            _const_spec((1, D)),
            _const_spec((D, n_in)),
        ],
        out_specs=[
            pl.BlockSpec((1, tile, D), lambda b, i: (b, i, 0)),
            pl.BlockSpec((1, tile, D_POOL), lambda b, i: (b, i, 0)),
            hg_spec, hg_spec, hg_spec,
        ],
        out_shape=[
            jax.ShapeDtypeStruct((B, T, D), F32),
            jax.ShapeDtypeStruct((B, T, D_POOL), F32),
            jax.ShapeDtypeStruct(hg_shape, BF16),
            jax.ShapeDtypeStruct(hg_shape, BF16),
            jax.ShapeDtypeStruct(hg_shape, BF16),
        ],
        compiler_params=_params(2),
        name="inproj",
    )(x, ln_g, ln_b, w_in_bf16)


def _attend(q2, keys, vals, bias):
    m = q2.shape[0]
    first = lax.broadcasted_iota(jnp.int32, (m, PAIR), 1) < HEAD_DIM
    zero = jnp.zeros_like(q2)
    qs = jnp.concatenate([jnp.where(first, q2, zero), jnp.where(first, zero, q2)], axis=0)
    s = lax.dot_general(qs, keys, (((1,), (1,)), ((), ())), preferred_element_type=F32) + bias
    e = jnp.exp(s - jnp.max(s, axis=-1, keepdims=True))
    denom = jnp.sum(e, axis=-1, keepdims=True)
    o = jnp.dot(e.astype(BF16), vals, preferred_element_type=F32) / denom
    return jnp.where(first, o[:m], o[m:])


def _attn_body(q_ref, k_ref, v_ref, qm_ref, km_ref, vm_ref, bias_ref, y_ref, ym_ref, *, rows):
    rb = pl.program_id(2)
    n_real = KEY_ROWS * GRID_W
    pad = jnp.zeros((GRID_W - N_META, PAIR), BF16)

    def key_set(ref, meta_ref, k_off, ls):
        return jnp.concatenate(
            [ref[0, 0, pl.ds(k_off, n_real), ls], meta_ref[0, 0, :, ls], pad], axis=0)

    for blk in range(ROWS_PER_STEP // ROWS_PER_BLOCK):
        r0 = rb * ROWS_PER_STEP + blk * ROWS_PER_BLOCK
        ks = jnp.clip(r0 - KH // 2, 0, rows - KEY_ROWS)
        variant = jnp.where(r0 == 0, 0, jnp.where(r0 == rows - ROWS_PER_BLOCK, 2, 1))
        k_off = pl.multiple_of(ks * GRID_W, GRID_W)
        qs = slice(blk * BLOCK_Q, (blk + 1) * BLOCK_Q)
        for p in range(PAIRS_PER_GROUP):
            ls = slice(p * PAIR, (p + 1) * PAIR)
            out = _attend(q_ref[0, 0, qs, ls], key_set(k_ref, km_ref, k_off, ls),
                          key_set(v_ref, vm_ref, k_off, ls), bias_ref[variant, p])
            y_ref[0, qs, ls] = out.astype(BF16)

    @pl.when(rb == 0)
    def _():
        for p in range(PAIRS_PER_GROUP):
            ls = slice(p * PAIR, (p + 1) * PAIR)
            bias = jnp.concatenate(
                [jnp.broadcast_to(bias_ref[0, p, 0:1, :], (N_META, ATTN_KEYS)),
                 jnp.broadcast_to(bias_ref[0, p, BLOCK_Q:BLOCK_Q + 1, :], (N_META, ATTN_KEYS))],
                axis=0)
            out = _attend(qm_ref[0, 0, :, ls], key_set(k_ref, km_ref, 0, ls),
                          key_set(v_ref, vm_ref, 0, ls), bias)
            ym_ref[0, :, ls] = out.astype(BF16)


def _attention(q, k, v, qm, km, vm, bias_tab):
    B, _, T, _ = q.shape
    rows = T // GRID_W
    assert rows % ROWS_PER_STEP == 0 and rows >= KEY_ROWS + 1
    q_rows = ROWS_PER_STEP * GRID_W
    meta_spec = pl.BlockSpec((1, 1, N_META, HEAD_GROUP), lambda b, g, i: (0, g, 0, 0))
    kv_spec = pl.BlockSpec((1, 1, T, HEAD_GROUP), lambda b, g, i: (b, g, 0, 0))
    return pl.pallas_call(
        functools.partial(_attn_body, rows=rows),
        grid=(B, N_HEAD_GROUPS, rows // ROWS_PER_STEP),
        in_specs=[
            pl.BlockSpec((1, 1, q_rows, HEAD_GROUP), lambda b, g, i: (b, g, i, 0)),
            kv_spec, kv_spec,
            meta_spec, meta_spec, meta_spec,
            pl.BlockSpec((N_VARIANTS, PAIRS_PER_GROUP, 2 * BLOCK_Q, ATTN_KEYS),
                         lambda b, g, i: (0, g, 0, 0)),
        ],
        out_specs=[
            pl.BlockSpec((1, q_rows, HEAD_GROUP), lambda b, g, i: (b, i, g)),
            pl.BlockSpec((1, N_META, HEAD_GROUP), lambda b, g, i: (b, 0, g)),
        ],
        out_shape=[
            jax.ShapeDtypeStruct((B, T, D_ATTN), BF16),
            jax.ShapeDtypeStruct((B, N_META, D_ATTN), BF16),
        ],
        compiler_params=_params(3),
        name="attention",
    )(q, k, v, qm, km, vm, bias_tab)


def _bias_table(rpb, meta_bias):
    c = np.arange(GRID_W)
    kc = np.arange(GRID_W)
    cs = np.clip(c - KW // 2, 0, GRID_W - KW)
    valid = (kc[None, :] >= cs[:, None]) & (kc[None, :] < cs[:, None] + KW)
    dc = kc[None, :] - c[:, None] + (KW - 1)
    onehot = ((dc[None] == np.arange(2 * KW - 1)[:, None, None]) & valid[None]).astype(np.float32)
    toep = jnp.einsum('hrj,jck->hcrk', rpb.astype(F32), onehot,
                      precision=lax.Precision.HIGHEST)
    toep = jnp.where(valid[None, :, None, :], toep, MASK_VALUE)
    variants = []
    for offset, first_key_row in ((0, lambda ri: 0), (KH // 2, lambda ri: ri),
                                  (KEY_ROWS - ROWS_PER_BLOCK, lambda ri: KH // 2 - 1)):
        per_row = []
        for ri in range(ROWS_PER_BLOCK):
            lo = first_key_row(ri)
            dr = lo - offset - ri + (KH - 1)
            piece = jnp.pad(toep[:, :, dr:dr + KH, :], ((0, 0), (0, 0), (lo, KEY_ROWS - KH - lo), (0, 0)),
                            constant_values=MASK_VALUE)
            per_row.append(piece.reshape(N_HEADS, GRID_W, KEY_ROWS * GRID_W))
        variants.append(jnp.stack(per_row, axis=1))
    tab = jnp.stack(variants).reshape(N_VARIANTS, N_HEADS, BLOCK_Q, KEY_ROWS * GRID_W)
    meta = jnp.pad(meta_bias.astype(F32), ((0, 0), (0, GRID_W - N_META)), constant_values=MASK_VALUE)
    meta = jnp.broadcast_to(meta[None, :, None, :], (N_VARIANTS, N_HEADS, BLOCK_Q, GRID_W))
    tab = jnp.concatenate([tab, meta], axis=-1)
    return tab.reshape(N_VARIANTS, N_HEADS // 2, 2 * BLOCK_Q, ATTN_KEYS)


def _pool_mixer(ext_ref, n, p0, seq_len, u, wp_ref, ps_ref):
    p = p0 + lax.broadcasted_iota(jnp.int32, (n, 1), 0)
    outs = []
    for g, w in enumerate(POOL_WINDOWS):
        cs = slice(g * POOL_GROUP_DIM, (g + 1) * POOL_GROUP_DIM)
        start = HALO - w // 2
        s = ext_ref[pl.ds(start, n), cs]
        for j in range(1, w):
            s = s + ext_ref[pl.ds(start + j, n), cs]
        cnt = (jnp.minimum(p + w // 2, seq_len) - jnp.maximum(p - w // 2, 0)).astype(F32)
        m = s / cnt - u[:, cs]
        y = jnp.dot(m.astype(BF16), wp_ref[g], preferred_element_type=F32) * ps_ref[:, cs]
        outs.append(y.astype(BF16))
    return jnp.concatenate(outs, axis=1)


def _mix_body(u_ref, up_ref, un_ref, um_ref, ya_ref, yam_ref, h_ref, hm_ref,
              wp_ref, ps_ref, wo_ref, g_ref, b_ref, h1_ref, h1m_ref, ext_ref, extm_ref,
              *, tile, seq_len):
    i = pl.program_id(1)
    last = pl.num_programs(1) - 1
    u = u_ref[0]
    um = um_ref[0]
    prev = jnp.where(i == 0, um[N_META - HALO:], up_ref[0])
    nxt = jnp.where(i == last, jnp.zeros((HALO, D_POOL), F32), un_ref[0])
    ext_ref[...] = jnp.concatenate([prev, u, nxt], axis=0)
    y_pool = _pool_mixer(ext_ref, tile, N_META + i * tile, seq_len, u, wp_ref, ps_ref)
    cat = jnp.concatenate([y_pool, ya_ref[0]], axis=1)
    mix = jnp.dot(cat, wo_ref[...], preferred_element_type=F32)
    h1_ref[0] = _layernorm(ALPHA * h_ref[0] + mix, g_ref[...], b_ref[...])

    @pl.when(i == 0)
    def _():
        extm_ref[...] = jnp.concatenate(
            [jnp.zeros((HALO, D_POOL), F32), um, u[:HALO]], axis=0)
        y_pool_m = _pool_mixer(extm_ref, N_META, 0, seq_len, um, wp_ref, ps_ref)
        cat_m = jnp.concatenate([y_pool_m, yam_ref[0]], axis=1)
        mix_m = jnp.dot(cat_m, wo_ref[...], preferred_element_type=F32)
        h1m_ref[0] = _layernorm(ALPHA * hm_ref[0] + mix_m, g_ref[...], b_ref[...])


def _halo_specs(tile, n_tokens, width):
    per_tile = tile // HALO
    n_blocks = n_tokens // HALO
    prev = pl.BlockSpec((1, HALO, width),
                        lambda b, i, *_: (b, jnp.maximum(i * per_tile - 1, 0), 0))
    nxt = pl.BlockSpec((1, HALO, width),
                       lambda b, i, *_: (b, jnp.minimum((i + 1) * per_tile, n_blocks - 1), 0))
    return prev, nxt


def _mix(u, um, ya, yam, h, hm, w_pool_bf16, pool_scale, w_out_bf16, ln_g, ln_b, tile):
    B, T, D = h.shape
    prev_spec, next_spec = _halo_specs(tile, T, D_POOL)
    meta = lambda width: pl.BlockSpec((1, N_META, width), lambda b, i: (0, 0, 0))
    meta_b = lambda width: pl.BlockSpec((1, N_META, width), lambda b, i: (b, 0, 0))
    return pl.pallas_call(
        functools.partial(_mix_body, tile=tile, seq_len=N_META + T),
        grid=(B, T // tile),
        in_specs=[
            pl.BlockSpec((1, tile, D_POOL), lambda b, i: (b, i, 0)),
            prev_spec, next_spec,
            meta(D_POOL),
            pl.BlockSpec((1, tile, D_ATTN), lambda b, i: (b, i, 0)),
            meta_b(D_ATTN),
            pl.BlockSpec((1, tile, D), lambda b, i: (b, i, 0)),
            meta(D),
            _const_spec(w_pool_bf16.shape),
            _const_spec((1, D_POOL)),
            _const_spec((D, D)),
            _const_spec((1, D)),
            _const_spec((1, D)),
        ],
        out_specs=[
            pl.BlockSpec((1, tile, D), lambda b, i: (b, i, 0)),
            meta_b(D),
        ],
        out_shape=[
            jax.ShapeDtypeStruct((B, T, D), F32),
            jax.ShapeDtypeStruct((B, N_META, D), F32),
        ],
        scratch_shapes=[
            pltpu.VMEM((tile + 2 * HALO, D_POOL), F32),
            pltpu.VMEM((N_META + 2 * HALO, D_POOL), F32),
        ],
        compiler_params=_params(2),
        name="mix",
    )(u, u, u, um, ya, yam, h, hm, w_pool_bf16, pool_scale, w_out_bf16, ln_g, ln_b)


def _ffn_body(h1_ref, hp_ref, hn_ref, h1m_ref, wa_ref, wg_ref, ba_ref, bg_ref,
              cw_ref, cb_ref, wdp_ref, wdl_ref, g_ref, b_ref, o_ref,
              lhs_ref, z_ref, acc_ref, act0_ref, act1_ref, *, tile):
    i = pl.program_id(1)
    c = pl.program_id(2)
    last_tile = pl.num_programs(1) - 1
    last_chunk = pl.num_programs(2) - 1
    n_sub = FF_CHUNK // FF_SUB

    @pl.when(c == 0)
    def _():
        prev = jnp.where(i == 0, h1m_ref[0, N_META - HALO:, :], hp_ref[0])
        lhs_ref[...] = jnp.concatenate([prev, h1_ref[0], hn_ref[0]], axis=0).astype(BF16)
        acc_ref[...] = jnp.zeros_like(acc_ref)
        act1_ref[...] = jnp.zeros_like(act1_ref)

    is_seq_end = i == last_tile
    sqrt_half = np.sqrt(0.5).astype(np.float32)

    def conv(half, s, bias_ref):
        cs = slice(s * FF_SUB, (s + 1) * FF_SUB)
        z = z_ref.at[half * n_sub + s]
        taps = [cw_ref[half, j:j + 1, cs] for j in range(3)]
        const = (taps[0] + taps[1] + taps[2]) * bias_ref[:, cs] + cb_ref[half, :, cs]
        return (z[pl.ds(HALO - 1, tile), :] * taps[0] + z[pl.ds(HALO, tile), :] * taps[1]
                + z[pl.ds(HALO + 1, tile), :] * taps[2] + const)

    def step(act_cur, act_prev):
        lhs = lhs_ref[...]
        for s in range(n_sub):
            cs = slice(s * FF_SUB, (s + 1) * FF_SUB)
            for half, (w_ref, bias_ref) in enumerate(((wa_ref, ba_ref), (wg_ref, bg_ref))):
                z = jnp.dot(lhs, w_ref[:, cs], preferred_element_type=F32)
                z_ref[half * n_sub + s] = z
                z_ref[half * n_sub + s, HALO + tile:, :] = jnp.where(
                    is_seq_end, -bias_ref[:, cs], z[HALO + tile:, :])
        acc_ref[...] += jnp.dot(act_prev[...], wdp_ref[...], preferred_element_type=F32)
        for s in range(n_sub):
            a = conv(0, s, ba_ref)
            g = conv(1, s, bg_ref)
            act = a * (0.5 * g * (1.0 + lax.erf(g * sqrt_half)))
            act_cur[:, s * FF_SUB:(s + 1) * FF_SUB] = act.astype(BF16)

        @pl.when(c == last_chunk)
        def _():
            total = acc_ref[...] + jnp.dot(act_cur[...], wdl_ref[...],
                                           preferred_element_type=F32)
            o_ref[0] = _layernorm(ALPHA * h1_ref[0] + total, g_ref[...], b_ref[...])

    @pl.when(c % 2 == 0)
    def _():
        step(act0_ref, act1_ref)

    @pl.when(c % 2 == 1)
    def _():
        step(act1_ref, act0_ref)


def _ffn(h1, h1m, w_up_p, b_up_p, conv_w_p, conv_b_p, w_down_p, ln_g, ln_b, tile):
    B, T, D = h1.shape
    prev_spec, next_spec = _halo_specs(tile, T, D)
    col_a = lambda rows: pl.BlockSpec((rows, FF_CHUNK), lambda b, i, c: (0, c))
    col_g = lambda rows: pl.BlockSpec((rows, FF_CHUNK), lambda b, i, c: (0, N_FF_CHUNKS + c))
    const3 = lambda shape: pl.BlockSpec(shape, lambda b, i, c: (0,) * len(shape),
                                        pipeline_mode=pl.Buffered(1))
    w_down_spec = lambda chunk: pl.BlockSpec((FF_CHUNK, D), lambda b, i, c: (chunk(c), 0))
    return pl.pallas_call(
        functools.partial(_ffn_body, tile=tile),
        grid=(B, T // tile, N_FF_CHUNKS),
        in_specs=[
            pl.BlockSpec((1, tile, D), lambda b, i, c: (b, i, 0)),
            prev_spec, next_spec,
            pl.BlockSpec((1, N_META, D), lambda b, i, c: (b, 0, 0)),
            col_a(D), col_g(D),
            col_a(1), col_g(1),
            pl.BlockSpec((2, 3, FF_CHUNK), lambda b, i, c: (0, 0, c)),
            pl.BlockSpec((2, 1, FF_CHUNK), lambda b, i, c: (0, 0, c)),
            w_down_spec(lambda c: jnp.maximum(c - 1, 0)),
            w_down_spec(lambda c: N_FF_CHUNKS - 1),
            const3((1, D)), const3((1, D)),
        ],
        out_specs=pl.BlockSpec((1, tile, D), lambda b, i, c: (b, i, 0)),
        out_shape=jax.ShapeDtypeStruct((B, T, D), F32),
        scratch_shapes=[
            pltpu.VMEM((tile + 2 * HALO, D), BF16),
            pltpu.VMEM((2 * FF_CHUNK // FF_SUB, tile + 2 * HALO, FF_SUB), F32),
            pltpu.VMEM((tile, D), F32),
            pltpu.VMEM((tile, FF_CHUNK), BF16),
            pltpu.VMEM((tile, FF_CHUNK), BF16),
        ],
        compiler_params=_params(3),
        name="ffn",
    )(h1, h1, h1, h1m, w_up_p, w_up_p, b_up_p, b_up_p, conv_w_p, conv_b_p, w_down_p, w_down_p,
      ln_g, ln_b)


def _pad_ff_cols(a):
    pad = [(0, 0)] * (a.ndim - 1) + [(0, D_FF_PAD - D_FF)]
    return jnp.concatenate([jnp.pad(a[..., :D_FF], pad), jnp.pad(a[..., D_FF:], pad)], axis=-1)


def _split_halves(a):
    return jnp.transpose(a.reshape(a.shape[0], 2, D_FF_PAD), (1, 0, 2))


def _encode(x, meta, consts):
    (ln_in_g, ln_in_b, w_in, w_pool, pool_scale, bias_tab, w_out, ln1_g, ln1_b,
     w_up, b_up, conv_w, conv_b, w_down, ln2_g, ln2_b) = consts
    hm, um, qm, km, vm = meta
    h, u, q, k, v = _inproj(x, ln_in_g, ln_in_b, w_in, TOKEN_TILE)
    ya, yam = _attention(q, k, v, qm, km, vm, bias_tab)
    h1, h1m = _mix(u, um, ya, yam, h, hm, w_pool, pool_scale, w_out, ln1_g, ln1_b, TOKEN_TILE)
    return _ffn(h1, h1m, w_up, b_up, conv_w, conv_b, w_down, ln2_g, ln2_b, TOKEN_TILE)


def kernel(x_prompt, x_sample, meta_tokens, ln_in_g, ln_in_b, w_in, w_pool, pool_scale, rpb,
           meta_bias, w_out, ln1_g, ln1_b, w_up, b_up, conv_w, conv_b, w_down, ln2_g, ln2_b):
    row = lambda a: a.reshape(1, -1).astype(F32)
    bias_tab = _bias_table(rpb[0], meta_bias[0])
    w_down_p = jnp.pad(w_down[0].astype(BF16), ((0, D_FF_PAD - D_FF), (0, 0)))
    consts = (
        row(ln_in_g), row(ln_in_b), w_in[0].astype(BF16), w_pool[0].astype(BF16),
        row(pool_scale[0]), bias_tab, w_out[0].astype(BF16), row(ln1_g[0]), row(ln1_b[0]),
        _pad_ff_cols(w_up[0].astype(BF16)), _pad_ff_cols(row(b_up[0])),
        _split_halves(_pad_ff_cols(conv_w[0].astype(F32))),
        _split_halves(_pad_ff_cols(row(conv_b[0]))), w_down_p,
        row(ln2_g[0]), row(ln2_b[0]),
    )
    meta = _inproj(meta_tokens[None].astype(F32), consts[0], consts[1], consts[2], N_META)
    y_prompt = _encode(x_prompt, meta, consts)
    y_sample = _encode(x_sample, meta, consts)
    return (y_prompt, y_sample)
```

```python
import functools

import numpy as np
import jax
import jax.numpy as jnp
from jax import lax
from jax.experimental import pallas as pl
from jax.experimental.pallas import tpu as pltpu

F32 = jnp.float32
BF16 = jnp.bfloat16

D_MODEL = 2048
N_META = 16
GRID_W = 64
KH = 8
KW = 16
D_POOL = D_MODEL // 2
D_ATTN = D_MODEL - D_POOL
POOL_WINDOWS = (2, 4, 8, 16)
POOL_GROUP_DIM = D_POOL // len(POOL_WINDOWS)
HEAD_DIM = 64
N_HEADS = D_ATTN // HEAD_DIM
D_FF = 5504
LN_EPS = 1e-5
DEPTH = 1
ALPHA = float((2 * DEPTH) ** 0.25)
QK_SCALE = HEAD_DIM ** -0.5
MASK_VALUE = -1e30

LANES = 128
SUBLANES = 8
PAIR = 2 * HEAD_DIM
HEAD_GROUP = 256
N_HEAD_GROUPS = D_ATTN // HEAD_GROUP
PAIRS_PER_GROUP = HEAD_GROUP // PAIR
HALO = SUBLANES
VMEM_LIMIT_BYTES = 56 * 1024 * 1024

TOKEN_TILE = 512
MIX_SUBTILES = 2
ROWS_PER_STEP = 8
ROWS_PER_BLOCK = 4
KEY_ROWS = ROWS_PER_BLOCK + KH - 1
ATTN_KEYS = (KEY_ROWS + 1) * GRID_W
BLOCK_Q = ROWS_PER_BLOCK * GRID_W
N_VARIANTS = 3
FF_CHUNK = 512
D_FF_PAD = -(-D_FF // FF_CHUNK) * FF_CHUNK
N_FF_CHUNKS = D_FF_PAD // FF_CHUNK
FF_SUB = 256


def _layernorm(x, g, b):
    mu = jnp.mean(x, axis=-1, keepdims=True)
    xc = x - mu
    var = jnp.mean(xc * xc, axis=-1, keepdims=True)
    return xc * lax.rsqrt(var + LN_EPS) * g + b


def _const_spec(shape):
    zeros = (0,) * len(shape)
    return pl.BlockSpec(shape, lambda *_: zeros, pipeline_mode=pl.Buffered(1))


def _params(n_axes, flags=None):
    return pltpu.CompilerParams(
        dimension_semantics=("arbitrary",) * n_axes,
        vmem_limit_bytes=VMEM_LIMIT_BYTES,
        flags=flags,
    )


def _inproj_body(x_ref, g_ref, b_ref, w_ref, h_ref, u_ref, q_ref, k_ref, v_ref):
    h = _layernorm(x_ref[0], g_ref[...], b_ref[...])
    h_ref[0] = h
    hb = h.astype(BF16)
    u_ref[0] = jnp.dot(hb, w_ref[:, 0:D_POOL], preferred_element_type=F32)
    for j in range(N_HEAD_GROUPS):
        lo = D_POOL + j * HEAD_GROUP
        q = jnp.dot(hb, w_ref[:, lo:lo + HEAD_GROUP], preferred_element_type=F32)
        q_ref[0, j] = (q * QK_SCALE).astype(BF16)
        lo += D_ATTN
        k_ref[0, j] = jnp.dot(hb, w_ref[:, lo:lo + HEAD_GROUP],
                              preferred_element_type=F32).astype(BF16)
        lo += D_ATTN
        v_ref[0, j] = jnp.dot(hb, w_ref[:, lo:lo + HEAD_GROUP],
                              preferred_element_type=F32).astype(BF16)


def _inproj(x, ln_g, ln_b, w_in_bf16, tile):
    B, T, D = x.shape
    n_in = w_in_bf16.shape[1]
    hg_shape = (B, N_HEAD_GROUPS, T, HEAD_GROUP)
    hg_spec = pl.BlockSpec((1, N_HEAD_GROUPS, tile, HEAD_GROUP), lambda b, i: (b, 0, i, 0))
    return pl.pallas_call(
        _inproj_body,
        grid=(B, T // tile),
        in_specs=[
            pl.BlockSpec((1, tile, D), lambda b, i: (b, i, 0)),
            _const_spec((1, D)),
            _const_spec((1, D)),
            _const_spec((D, n_in)),
        ],
        out_specs=[
            pl.BlockSpec((1, tile, D), lambda b, i: (b, i, 0)),
            pl.BlockSpec((1, tile, D_POOL), lambda b, i: (b, i, 0)),
            hg_spec, hg_spec, hg_spec,
        ],
        out_shape=[
            jax.ShapeDtypeStruct((B, T, D), F32),
            jax.ShapeDtypeStruct((B, T, D_POOL), F32),
            jax.ShapeDtypeStruct(hg_shape, BF16),
            jax.ShapeDtypeStruct(hg_shape, BF16),
            jax.ShapeDtypeStruct(hg_shape, BF16),
        ],
        compiler_params=_params(2),
        name="inproj",
    )(x, ln_g, ln_b, w_in_bf16)


def _attend(q2, keys, vals, bias):
    m = q2.shape[0]
    first = lax.broadcasted_iota(jnp.int32, (m, PAIR), 1) < HEAD_DIM
    zero = jnp.zeros_like(q2)
    qs = jnp.concatenate([jnp.where(first, q2, zero), jnp.where(first, zero, q2)], axis=0)
    s = lax.dot_general(qs, keys, (((1,), (1,)), ((), ())), preferred_element_type=F32) + bias
    e = jnp.exp(s - jnp.max(s, axis=-1, keepdims=True))
    denom = jnp.sum(e, axis=-1, keepdims=True)
    o = jnp.dot(e.astype(BF16), vals, preferred_element_type=F32) / denom
    return jnp.where(first, o[:m], o[m:])


def _attn_body(q_ref, k_ref, v_ref, qm_ref, km_ref, vm_ref, bias_ref, y_ref, ym_ref, *, rows):
    rb = pl.program_id(2)
    n_real = KEY_ROWS * GRID_W
    pad = jnp.zeros((GRID_W - N_META, PAIR), BF16)

    def key_set(ref, meta_ref, k_off, ls):
        return jnp.concatenate(
            [ref[0, 0, pl.ds(k_off, n_real), ls], meta_ref[0, 0, :, ls], pad], axis=0)

    for blk in range(ROWS_PER_STEP // ROWS_PER_BLOCK):
        r0 = rb * ROWS_PER_STEP + blk * ROWS_PER_BLOCK
        ks = jnp.clip(r0 - KH // 2, 0, rows - KEY_ROWS)
        variant = jnp.where(r0 == 0, 0, jnp.where(r0 == rows - ROWS_PER_BLOCK, 2, 1))
        k_off = pl.multiple_of(ks * GRID_W, GRID_W)
        qs = slice(blk * BLOCK_Q, (blk + 1) * BLOCK_Q)
        for p in range(PAIRS_PER_GROUP):
            ls = slice(p * PAIR, (p + 1) * PAIR)
            out = _attend(q_ref[0, 0, qs, ls], key_set(k_ref, km_ref, k_off, ls),
                          key_set(v_ref, vm_ref, k_off, ls), bias_ref[variant, p])
            y_ref[0, qs, ls] = out.astype(BF16)

    @pl.when(rb == 0)
    def _():
        for p in range(PAIRS_PER_GROUP):
            ls = slice(p * PAIR, (p + 1) * PAIR)
            bias = jnp.concatenate(
                [jnp.broadcast_to(bias_ref[0, p, 0:1, :], (N_META, ATTN_KEYS)),
                 jnp.broadcast_to(bias_ref[0, p, BLOCK_Q:BLOCK_Q + 1, :], (N_META, ATTN_KEYS))],
                axis=0)
            out = _attend(qm_ref[0, 0, :, ls], key_set(k_ref, km_ref, 0, ls),
                          key_set(v_ref, vm_ref, 0, ls), bias)
            ym_ref[0, :, ls] = out.astype(BF16)


def _attention(q, k, v, qm, km, vm, bias_tab):
    B, _, T, _ = q.shape
    rows = T // GRID_W
    assert rows % ROWS_PER_STEP == 0 and rows >= KEY_ROWS + 1
    q_rows = ROWS_PER_STEP * GRID_W
    meta_spec = pl.BlockSpec((1, 1, N_META, HEAD_GROUP), lambda b, g, i: (0, g, 0, 0))
    kv_spec = pl.BlockSpec((1, 1, T, HEAD_GROUP), lambda b, g, i: (b, g, 0, 0))
    return pl.pallas_call(
        functools.partial(_attn_body, rows=rows),
        grid=(B, N_HEAD_GROUPS, rows // ROWS_PER_STEP),
        in_specs=[
            pl.BlockSpec((1, 1, q_rows, HEAD_GROUP), lambda b, g, i: (b, g, i, 0)),
            kv_spec, kv_spec,
            meta_spec, meta_spec, meta_spec,
            pl.BlockSpec((N_VARIANTS, PAIRS_PER_GROUP, 2 * BLOCK_Q, ATTN_KEYS),
                         lambda b, g, i: (0, g, 0, 0)),
        ],
        out_specs=[
            pl.BlockSpec((1, q_rows, HEAD_GROUP), lambda b, g, i: (b, i, g)),
            pl.BlockSpec((1, N_META, HEAD_GROUP), lambda b, g, i: (b, 0, g)),
        ],
        out_shape=[
            jax.ShapeDtypeStruct((B, T, D_ATTN), BF16),
            jax.ShapeDtypeStruct((B, N_META, D_ATTN), BF16),
        ],
        compiler_params=_params(3),
        name="attention",
    )(q, k, v, qm, km, vm, bias_tab)


def _bias_table(rpb, meta_bias):
    c = np.arange(GRID_W)
    kc = np.arange(GRID_W)
    cs = np.clip(c - KW // 2, 0, GRID_W - KW)
    valid = (kc[None, :] >= cs[:, None]) & (kc[None, :] < cs[:, None] + KW)
    dc = kc[None, :] - c[:, None] + (KW - 1)
    onehot = ((dc[None] == np.arange(2 * KW - 1)[:, None, None]) & valid[None]).astype(np.float32)
    toep = jnp.einsum('hrj,jck->hcrk', rpb.astype(F32), onehot,
                      precision=lax.Precision.HIGHEST)
    toep = jnp.where(valid[None, :, None, :], toep, MASK_VALUE)
    variants = []
    for offset, first_key_row in ((0, lambda ri: 0), (KH // 2, lambda ri: ri),
                                  (KEY_ROWS - ROWS_PER_BLOCK, lambda ri: KH // 2 - 1)):
        per_row = []
        for ri in range(ROWS_PER_BLOCK):
            lo = first_key_row(ri)
            dr = lo - offset - ri + (KH - 1)
            piece = jnp.pad(toep[:, :, dr:dr + KH, :], ((0, 0), (0, 0), (lo, KEY_ROWS - KH - lo), (0, 0)),
                            constant_values=MASK_VALUE)
            per_row.append(piece.reshape(N_HEADS, GRID_W, KEY_ROWS * GRID_W))
        variants.append(jnp.stack(per_row, axis=1))
    tab = jnp.stack(variants).reshape(N_VARIANTS, N_HEADS, BLOCK_Q, KEY_ROWS * GRID_W)
    meta = jnp.pad(meta_bias.astype(F32), ((0, 0), (0, GRID_W - N_META)), constant_values=MASK_VALUE)
    meta = jnp.broadcast_to(meta[None, :, None, :], (N_VARIANTS, N_HEADS, BLOCK_Q, GRID_W))
    tab = jnp.concatenate([tab, meta], axis=-1)
    return tab.reshape(N_VARIANTS, N_HEADS // 2, 2 * BLOCK_Q, ATTN_KEYS)


def _pool_mixer(ext_ref, sum_ref, r0, n, p0, seq_len, wp_ref, ps_ref):
    p = p0 + lax.broadcasted_iota(jnp.int32, (n, 1), 0)
    outs = []
    for g, w in enumerate(POOL_WINDOWS):
        cs = slice(g * POOL_GROUP_DIM, (g + 1) * POOL_GROUP_DIM)
        span = n + 2 * HALO
        s = ext_ref[pl.ds(r0, span), cs]
        width = 1
        while width < w:
            s = s + pltpu.roll(s, span - width, axis=0)
            width *= 2
        sum_ref[:, cs] = s
        cnt = (jnp.minimum(p + w // 2, seq_len) - jnp.maximum(p - w // 2, 0)).astype(F32)
        m = sum_ref[pl.ds(HALO - w // 2, n), cs] / cnt - ext_ref[pl.ds(r0 + HALO, n), cs]
        y = jnp.dot(m.astype(BF16), wp_ref[g], preferred_element_type=F32) * ps_ref[:, cs]
        outs.append(y.astype(BF16))
    return jnp.concatenate(outs, axis=1)


def _mix_body(u_ref, up_ref, un_ref, um_ref, ya_ref, yam_ref, h_ref, hm_ref,
              wp_ref, ps_ref, wo_ref, g_ref, b_ref, h1_ref, h1m_ref,
              ext_ref, sum_ref, extm_ref, summ_ref, cat_ref, *, tile, seq_len):
    i = pl.program_id(1)
    last = pl.num_programs(1) - 1
    um = um_ref[0]
    prev = jnp.where(i == 0, um[N_META - HALO:], up_ref[0])
    nxt = jnp.where(i == last, jnp.zeros((HALO, D_POOL), F32), un_ref[0])
    ext_ref[...] = jnp.concatenate([prev, u_ref[0], nxt], axis=0)
    sub = tile // MIX_SUBTILES
    for r in range(MIX_SUBTILES):
        rows = slice(r * sub, (r + 1) * sub)
        cat_ref[rows, :D_POOL] = _pool_mixer(
            ext_ref, sum_ref.at[r], r * sub, sub, N_META + i * tile + r * sub, seq_len,
            wp_ref, ps_ref)
        cat_ref[rows, D_POOL:] = ya_ref[0, rows, :]
    for r in range(MIX_SUBTILES):
        rows = slice(r * sub, (r + 1) * sub)
        mix = jnp.dot(cat_ref[rows, :], wo_ref[...], preferred_element_type=F32)
        h1_ref[0, rows, :] = _layernorm(ALPHA * h_ref[0, rows, :] + mix, g_ref[...], b_ref[...])

    @pl.when(i == 0)
    def _():
        extm_ref[...] = jnp.concatenate(
            [jnp.zeros((HALO, D_POOL), F32), um, u_ref[0, :HALO, :]], axis=0)
        y_pool_m = _pool_mixer(extm_ref, summ_ref, 0, N_META, 0, seq_len, wp_ref, ps_ref)
        cat_m = jnp.concatenate([y_pool_m, yam_ref[0]], axis=1)
        mix_m = jnp.dot(cat_m, wo_ref[...], preferred_element_type=F32)
        h1m_ref[0] = _layernorm(ALPHA * hm_ref[0] + mix_m, g_ref[...], b_ref[...])


def _halo_specs(tile, n_tokens, width):
    per_tile = tile // HALO
    n_blocks = n_tokens // HALO
    prev = pl.BlockSpec((1, HALO, width),
                        lambda b, i, *_: (b, jnp.maximum(i * per_tile - 1, 0), 0))
    nxt = pl.BlockSpec((1, HALO, width),
                       lambda b, i, *_: (b, jnp.minimum((i + 1) * per_tile, n_blocks - 1), 0))
    return prev, nxt


def _mix(u, um, ya, yam, h, hm, w_pool_bf16, pool_scale, w_out_bf16, ln_g, ln_b, tile):
    B, T, D = h.shape
    prev_spec, next_spec = _halo_specs(tile, T, D_POOL)
    meta = lambda width: pl.BlockSpec((1, N_META, width), lambda b, i: (0, 0, 0))
    meta_b = lambda width: pl.BlockSpec((1, N_META, width), lambda b, i: (b, 0, 0))
    return pl.pallas_call(
        functools.partial(_mix_body, tile=tile, seq_len=N_META + T),
        grid=(B, T // tile),
        in_specs=[
            pl.BlockSpec((1, tile, D_POOL), lambda b, i: (b, i, 0)),
            prev_spec, next_spec,
            meta(D_POOL),
            pl.BlockSpec((1, tile, D_ATTN), lambda b, i: (b, i, 0)),
            meta_b(D_ATTN),
            pl.BlockSpec((1, tile, D), lambda b, i: (b, i, 0)),
            meta(D),
            _const_spec(w_pool_bf16.shape),
            _const_spec((1, D_POOL)),
            _const_spec((D, D)),
            _const_spec((1, D)),
            _const_spec((1, D)),
        ],
        out_specs=[
            pl.BlockSpec((1, tile, D), lambda b, i: (b, i, 0)),
            meta_b(D),
        ],
        out_shape=[
            jax.ShapeDtypeStruct((B, T, D), F32),
            jax.ShapeDtypeStruct((B, N_META, D), F32),
        ],
        scratch_shapes=[
            pltpu.VMEM((tile + 2 * HALO, D_POOL), F32),
            pltpu.VMEM((MIX_SUBTILES, tile // MIX_SUBTILES + 2 * HALO, D_POOL), F32),
            pltpu.VMEM((N_META + 2 * HALO, D_POOL), F32),
            pltpu.VMEM((N_META + 2 * HALO, D_POOL), F32),
            pltpu.VMEM((tile, D), BF16),
        ],
        compiler_params=_params(2),
        name="mix",
    )(u, u, u, um, ya, yam, h, hm, w_pool_bf16, pool_scale, w_out_bf16, ln_g, ln_b)


def _ffn_body(h1_ref, hp_ref, hn_ref, h1m_ref, wa_ref, wg_ref, ba_ref, bg_ref,
              cw_ref, cb_ref, wd_ref, g_ref, b_ref, o_ref, lhs_ref, z_ref, acc_ref, *, tile):
    i = pl.program_id(1)
    c = pl.program_id(2)
    last_tile = pl.num_programs(1) - 1
    last_chunk = pl.num_programs(2) - 1
    n_sub = FF_CHUNK // FF_SUB

    @pl.when(c == 0)
    def _():
        prev = jnp.where(i == 0, h1m_ref[0, N_META - HALO:, :], hp_ref[0])
        lhs_ref[...] = jnp.concatenate([prev, h1_ref[0], hn_ref[0]], axis=0).astype(BF16)
        acc_ref[...] = jnp.zeros_like(acc_ref)

    is_seq_end = i == last_tile
    sqrt_half = np.sqrt(0.5).astype(np.float32)

    def conv(half, s, bias_ref, row0, n_rows):
        cs = slice(s * FF_SUB, (s + 1) * FF_SUB)
        z = z_ref.at[half * n_sub + s]
        taps = [cw_ref[half, j:j + 1, cs] for j in range(3)]
        const = (taps[0] + taps[1] + taps[2]) * bias_ref[:, cs] + cb_ref[half, :, cs]
        return (z[pl.ds(HALO - 1 + row0, n_rows), :] * taps[0]
                + z[pl.ds(HALO + row0, n_rows), :] * taps[1]
                + z[pl.ds(HALO + 1 + row0, n_rows), :] * taps[2] + const)

    def gated(s, row0, n_rows):
        a = conv(0, s, ba_ref, row0, n_rows)
        g = conv(1, s, bg_ref, row0, n_rows)
        return (a * (0.5 * g * (1.0 + lax.erf(g * sqrt_half)))).astype(BF16)

    lhs = lhs_ref[...]
    for s in range(n_sub):
        cs = slice(s * FF_SUB, (s + 1) * FF_SUB)
        for half, (w_ref, bias_ref) in enumerate(((wa_ref, ba_ref), (wg_ref, bg_ref))):
            z = jnp.dot(lhs, w_ref[:, cs], preferred_element_type=F32)
            z_ref[half * n_sub + s] = z
            z_ref[half * n_sub + s, HALO + tile:, :] = jnp.where(
                is_seq_end, -bias_ref[:, cs], z[HALO + tile:, :])

    down = None
    for s in range(n_sub - 1):
        part = jnp.dot(gated(s, 0, tile), wd_ref[s * FF_SUB:(s + 1) * FF_SUB, :],
                       preferred_element_type=F32)
        down = part if down is None else down + part
    half_rows = tile // 2
    wd_last = wd_ref[(n_sub - 1) * FF_SUB:, :]
    for r in range(2):
        rows = slice(r * half_rows, (r + 1) * half_rows)
        part = jnp.dot(gated(n_sub - 1, r * half_rows, half_rows), wd_last,
                       preferred_element_type=F32)
        acc_ref[rows, :] += part if down is None else down[rows] + part

    @pl.when(c == last_chunk)
    def _():
        o_ref[0] = _layernorm(ALPHA * h1_ref[0] + acc_ref[...], g_ref[...], b_ref[...])


def _ffn(h1, h1m, w_up_p, b_up_p, conv_w_p, conv_b_p, w_down_p, ln_g, ln_b, tile):
    B, T, D = h1.shape
    prev_spec, next_spec = _halo_specs(tile, T, D)
    col_a = lambda rows: pl.BlockSpec((rows, FF_CHUNK), lambda b, i, c: (0, c))
    col_g = lambda rows: pl.BlockSpec((rows, FF_CHUNK), lambda b, i, c: (0, N_FF_CHUNKS + c))
    const3 = lambda shape: pl.BlockSpec(shape, lambda b, i, c: (0,) * len(shape),
                                        pipeline_mode=pl.Buffered(1))
    return pl.pallas_call(
        functools.partial(_ffn_body, tile=tile),
        grid=(B, T // tile, N_FF_CHUNKS),
        in_specs=[
            pl.BlockSpec((1, tile, D), lambda b, i, c: (b, i, 0)),
            prev_spec, next_spec,
            pl.BlockSpec((1, N_META, D), lambda b, i, c: (b, 0, 0)),
            col_a(D), col_g(D),
            col_a(1), col_g(1),
            pl.BlockSpec((2, 3, FF_CHUNK), lambda b, i, c: (0, 0, c)),
            pl.BlockSpec((2, 1, FF_CHUNK), lambda b, i, c: (0, 0, c)),
            pl.BlockSpec((FF_CHUNK, D), lambda b, i, c: (c, 0)),
            const3((1, D)), const3((1, D)),
        ],
        out_specs=pl.BlockSpec((1, tile, D), lambda b, i, c: (b, i, 0)),
        out_shape=jax.ShapeDtypeStruct((B, T, D), F32),
        scratch_shapes=[
            pltpu.VMEM((tile + 2 * HALO, D), BF16),
            pltpu.VMEM((2 * FF_CHUNK // FF_SUB, tile + 2 * HALO, FF_SUB), F32),
            pltpu.VMEM((tile, D), F32),
        ],
        compiler_params=_params(3),
        name="ffn",
    )(h1, h1, h1, h1m, w_up_p, w_up_p, b_up_p, b_up_p, conv_w_p, conv_b_p, w_down_p, ln_g, ln_b)


def _pad_ff_cols(a):
    pad = [(0, 0)] * (a.ndim - 1) + [(0, D_FF_PAD - D_FF)]
    return jnp.concatenate([jnp.pad(a[..., :D_FF], pad), jnp.pad(a[..., D_FF:], pad)], axis=-1)


def _split_halves(a):
    return jnp.transpose(a.reshape(a.shape[0], 2, D_FF_PAD), (1, 0, 2))


def _encode(x, meta, consts):
    (ln_in_g, ln_in_b, w_in, w_pool, pool_scale, bias_tab, w_out, ln1_g, ln1_b,
     w_up, b_up, conv_w, conv_b, w_down, ln2_g, ln2_b) = consts
    hm, um, qm, km, vm = meta
    h, u, q, k, v = _inproj(x, ln_in_g, ln_in_b, w_in, TOKEN_TILE)
    ya, yam = _attention(q, k, v, qm, km, vm, bias_tab)
    h1, h1m = _mix(u, um, ya, yam, h, hm, w_pool, pool_scale, w_out, ln1_g, ln1_b, TOKEN_TILE)
    return _ffn(h1, h1m, w_up, b_up, conv_w, conv_b, w_down, ln2_g, ln2_b, TOKEN_TILE)


def kernel(x_prompt, x_sample, meta_tokens, ln_in_g, ln_in_b, w_in, w_pool, pool_scale, rpb,
           meta_bias, w_out, ln1_g, ln1_b, w_up, b_up, conv_w, conv_b, w_down, ln2_g, ln2_b):
    row = lambda a: a.reshape(1, -1).astype(F32)
    bias_tab = _bias_table(rpb[0], meta_bias[0])
    w_down_p = jnp.pad(w_down[0].astype(BF16), ((0, D_FF_PAD - D_FF), (0, 0)))
    consts = (
        row(ln_in_g), row(ln_in_b), w_in[0].astype(BF16), w_pool[0].astype(BF16),
        row(pool_scale[0]), bias_tab, w_out[0].astype(BF16), row(ln1_g[0]), row(ln1_b[0]),
        _pad_ff_cols(w_up[0].astype(BF16)), _pad_ff_cols(row(b_up[0])),
        _split_halves(_pad_ff_cols(conv_w[0].astype(F32))),
        _split_halves(_pad_ff_cols(row(conv_b[0]))), w_down_p,
        row(ln2_g[0]), row(ln2_b[0]),
    )
    meta = _inproj(meta_tokens[None].astype(F32), consts[0], consts[1], consts[2], N_META)
    y_prompt = _encode(x_prompt, meta, consts)
    y_sample = _encode(x_sample, meta, consts)
    return (y_prompt, y_sample)
```

```python
import functools

import numpy as np
import jax
import jax.numpy as jnp
from jax import lax
from jax.experimental import pallas as pl
from jax.experimental.pallas import tpu as pltpu

F32 = jnp.float32
BF16 = jnp.bfloat16

D_MODEL = 2048
N_META = 16
GRID_W = 64
KH = 8
KW = 16
D_POOL = D_MODEL // 2
D_ATTN = D_MODEL - D_POOL
POOL_WINDOWS = (2, 4, 8, 16)
POOL_GROUP_DIM = D_POOL // len(POOL_WINDOWS)
HEAD_DIM = 64
N_HEADS = D_ATTN // HEAD_DIM
D_FF = 5504
LN_EPS = 1e-5
DEPTH = 1
ALPHA = float((2 * DEPTH) ** 0.25)
QK_SCALE = HEAD_DIM ** -0.5
LOG2_E = float(np.log2(np.e))
MASK_VALUE = -1e30

LANES = 128
SUBLANES = 8
PAIR = 2 * HEAD_DIM
HEAD_GROUP = 256
N_HEAD_GROUPS = D_ATTN // HEAD_GROUP
PAIRS_PER_GROUP = HEAD_GROUP // PAIR
HALO = SUBLANES
VMEM_LIMIT_BYTES = 56 * 1024 * 1024

TOKEN_TILE = 512
MIX_SUBTILES = 2
ROWS_PER_STEP = 16
ROWS_PER_BLOCK = 4
KEY_ROWS = ROWS_PER_BLOCK + KH - 1
ATTN_KEYS = (KEY_ROWS + 1) * GRID_W
BLOCK_Q = ROWS_PER_BLOCK * GRID_W
N_VARIANTS = 3
FF_CHUNK = 512
D_FF_PAD = -(-D_FF // FF_CHUNK) * FF_CHUNK
N_FF_CHUNKS = D_FF_PAD // FF_CHUNK
FF_SUB = 256


def _layernorm(x, g, b):
    mu = jnp.mean(x, axis=-1, keepdims=True)
    xc = x - mu
    var = jnp.mean(xc * xc, axis=-1, keepdims=True)
    return xc * lax.rsqrt(var + LN_EPS) * g + b


def _const_spec(shape):
    zeros = (0,) * len(shape)
    return pl.BlockSpec(shape, lambda *_: zeros, pipeline_mode=pl.Buffered(1))


def _params(n_axes, flags=None):
    return pltpu.CompilerParams(
        dimension_semantics=("arbitrary",) * n_axes,
        vmem_limit_bytes=VMEM_LIMIT_BYTES,
        flags=flags,
    )


def _inproj_body(x_ref, g_ref, b_ref, w_ref, h_ref, u_ref, q_ref, k_ref, v_ref):
    h = _layernorm(x_ref[0], g_ref[...], b_ref[...])
    h_ref[0] = h
    hb = h.astype(BF16)
    u_ref[0] = jnp.dot(hb, w_ref[:, 0:D_POOL], preferred_element_type=F32)
    for j in range(N_HEAD_GROUPS):
        lo = D_POOL + j * HEAD_GROUP
        q = jnp.dot(hb, w_ref[:, lo:lo + HEAD_GROUP], preferred_element_type=F32)
        q_ref[0, j] = (q * (QK_SCALE * LOG2_E)).astype(BF16)
        lo += D_ATTN
        k_ref[0, j] = jnp.dot(hb, w_ref[:, lo:lo + HEAD_GROUP],
                              preferred_element_type=F32).astype(BF16)
        lo += D_ATTN
        v_ref[0, j] = jnp.dot(hb, w_ref[:, lo:lo + HEAD_GROUP],
                              preferred_element_type=F32).astype(BF16)


def _inproj(x, ln_g, ln_b, w_in_bf16, tile):
    B, T, D = x.shape
    n_in = w_in_bf16.shape[1]
    hg_shape = (B, N_HEAD_GROUPS, T, HEAD_GROUP)
    hg_spec = pl.BlockSpec((1, N_HEAD_GROUPS, tile, HEAD_GROUP), lambda b, i: (b, 0, i, 0))
    return pl.pallas_call(
        _inproj_body,
        grid=(B, T // tile),
        in_specs=[
            pl.BlockSpec((1, tile, D), lambda b, i: (b, i, 0)),
            _const_spec((1, D)),
            _const_spec((1, D)),
            _const_spec((D, n_in)),
        ],
        out_specs=[
            pl.BlockSpec((1, tile, D), lambda b, i: (b, i, 0)),
            pl.BlockSpec((1, tile, D_POOL), lambda b, i: (b, i, 0)),
            hg_spec, hg_spec, hg_spec,
        ],
        out_shape=[
            jax.ShapeDtypeStruct((B, T, D), F32),
            jax.ShapeDtypeStruct((B, T, D_POOL), F32),
            jax.ShapeDtypeStruct(hg_shape, BF16),
            jax.ShapeDtypeStruct(hg_shape, BF16),
            jax.ShapeDtypeStruct(hg_shape, BF16),
        ],
        compiler_params=_params(2),
        name="inproj",
    )(x, ln_g, ln_b, w_in_bf16)


def _attend(q2, keys, vals, bias):
    m = q2.shape[0]
    first = lax.broadcasted_iota(jnp.int32, (m, PAIR), 1) < HEAD_DIM
    zero = jnp.zeros_like(q2)
    qs = jnp.concatenate([jnp.where(first, q2, zero), jnp.where(first, zero, q2)], axis=0)
    s = lax.dot_general(qs, keys, (((1,), (1,)), ((), ())), preferred_element_type=F32) + bias
    e = jnp.exp2(s - jnp.max(s, axis=-1, keepdims=True)).astype(BF16)
    vals_ones = jnp.concatenate([vals, jnp.ones_like(vals)], axis=1)
    o = jnp.dot(e, vals_ones, preferred_element_type=F32)
    o = o[:, :PAIR] / o[:, PAIR:]
    return jnp.where(first, o[:m], o[m:])


def _attn_body(q_ref, k_ref, v_ref, qm_ref, km_ref, vm_ref, bias_ref, y_ref, ym_ref, *, rows):
    rb = pl.program_id(2)
    n_real = KEY_ROWS * GRID_W
    pad = jnp.zeros((GRID_W - N_META, PAIR), BF16)

    def key_set(ref, meta_ref, k_off, ls):
        return jnp.concatenate(
            [ref[0, 0, pl.ds(k_off, n_real), ls], meta_ref[0, 0, :, ls], pad], axis=0)

    for blk in range(ROWS_PER_STEP // ROWS_PER_BLOCK):
        r0 = rb * ROWS_PER_STEP + blk * ROWS_PER_BLOCK
        ks = jnp.clip(r0 - KH // 2, 0, rows - KEY_ROWS)
        variant = jnp.where(r0 == 0, 0, jnp.where(r0 == rows - ROWS_PER_BLOCK, 2, 1))
        k_off = pl.multiple_of(ks * GRID_W, GRID_W)
        qs = slice(blk * BLOCK_Q, (blk + 1) * BLOCK_Q)
        for p in range(PAIRS_PER_GROUP):
            ls = slice(p * PAIR, (p + 1) * PAIR)
            out = _attend(q_ref[0, 0, qs, ls], key_set(k_ref, km_ref, k_off, ls),
                          key_set(v_ref, vm_ref, k_off, ls), bias_ref[variant, p])
            y_ref[0, qs, ls] = out.astype(BF16)

    @pl.when(rb == 0)
    def _():
        for p in range(PAIRS_PER_GROUP):
            ls = slice(p * PAIR, (p + 1) * PAIR)
            bias = jnp.concatenate(
                [jnp.broadcast_to(bias_ref[0, p, 0:1, :], (N_META, ATTN_KEYS)),
                 jnp.broadcast_to(bias_ref[0, p, BLOCK_Q:BLOCK_Q + 1, :], (N_META, ATTN_KEYS))],
                axis=0)
            out = _attend(qm_ref[0, 0, :, ls], key_set(k_ref, km_ref, 0, ls),
                          key_set(v_ref, vm_ref, 0, ls), bias)
            ym_ref[0, :, ls] = out.astype(BF16)


def _attention(q, k, v, qm, km, vm, bias_tab):
    B, _, T, _ = q.shape
    rows = T // GRID_W
    assert rows % ROWS_PER_STEP == 0 and rows >= KEY_ROWS + 1
    q_rows = ROWS_PER_STEP * GRID_W
    meta_spec = pl.BlockSpec((1, 1, N_META, HEAD_GROUP), lambda b, g, i: (0, g, 0, 0))
    kv_spec = pl.BlockSpec((1, 1, T, HEAD_GROUP), lambda b, g, i: (b, g, 0, 0))
    return pl.pallas_call(
        functools.partial(_attn_body, rows=rows),
        grid=(B, N_HEAD_GROUPS, rows // ROWS_PER_STEP),
        in_specs=[
            pl.BlockSpec((1, 1, q_rows, HEAD_GROUP), lambda b, g, i: (b, g, i, 0)),
            kv_spec, kv_spec,
            meta_spec, meta_spec, meta_spec,
            pl.BlockSpec((N_VARIANTS, PAIRS_PER_GROUP, 2 * BLOCK_Q, ATTN_KEYS),
                         lambda b, g, i: (0, g, 0, 0)),
        ],
        out_specs=[
            pl.BlockSpec((1, q_rows, HEAD_GROUP), lambda b, g, i: (b, i, g)),
            pl.BlockSpec((1, N_META, HEAD_GROUP), lambda b, g, i: (b, 0, g)),
        ],
        out_shape=[
            jax.ShapeDtypeStruct((B, T, D_ATTN), BF16),
            jax.ShapeDtypeStruct((B, N_META, D_ATTN), BF16),
        ],
        compiler_params=_params(3),
        name="attention",
    )(q, k, v, qm, km, vm, bias_tab)


def _bias_table(rpb, meta_bias):
    c = np.arange(GRID_W)
    kc = np.arange(GRID_W)
    cs = np.clip(c - KW // 2, 0, GRID_W - KW)
    valid = (kc[None, :] >= cs[:, None]) & (kc[None, :] < cs[:, None] + KW)
    dc = kc[None, :] - c[:, None] + (KW - 1)
    onehot = ((dc[None] == np.arange(2 * KW - 1)[:, None, None]) & valid[None]).astype(np.float32)
    toep = jnp.einsum('hrj,jck->hcrk', rpb.astype(F32) * LOG2_E, onehot,
                      precision=lax.Precision.HIGHEST)
    toep = jnp.where(valid[None, :, None, :], toep, MASK_VALUE)
    front = KEY_ROWS - ROWS_PER_BLOCK + ROWS_PER_BLOCK - 1 - (KH - 1)
    toep = jnp.pad(toep, ((0, 0), (0, 0), (front, KEY_ROWS - KH), (0, 0)),
                   constant_values=MASK_VALUE)
    meta = jnp.pad(meta_bias.astype(F32) * LOG2_E, ((0, 0), (0, GRID_W - N_META)),
                   constant_values=MASK_VALUE)
    meta = jnp.broadcast_to(meta[:, None, None, :], (N_HEADS, GRID_W, 1, GRID_W))
    kr = np.arange(KEY_ROWS)
    variants = []
    for offset, first_key_row in ((0, lambda ri: 0), (KH // 2, lambda ri: ri),
                                  (KEY_ROWS - ROWS_PER_BLOCK, lambda ri: KH // 2 - 1)):
        per_row = []
        for ri in range(ROWS_PER_BLOCK):
            lo = first_key_row(ri)
            start = front - offset - ri + (KH - 1)
            in_window = ((kr >= lo) & (kr < lo + KH))[None, None, :, None]
            piece = jnp.where(in_window, toep[:, :, start:start + KEY_ROWS, :], MASK_VALUE)
            per_row.append(jnp.concatenate([piece, meta], axis=2))
        variants.append(jnp.stack(per_row, axis=1))
    return jnp.stack(variants).reshape(N_VARIANTS, N_HEADS // 2, 2 * BLOCK_Q, ATTN_KEYS)


def _pool_mixer(ext_ref, sum_ref, r0, n, p0, seq_len, wp_ref, ps_ref):
    p = p0 + lax.broadcasted_iota(jnp.int32, (n, 1), 0)
    outs = []
    for g, w in enumerate(POOL_WINDOWS):
        cs = slice(g * POOL_GROUP_DIM, (g + 1) * POOL_GROUP_DIM)
        span = n + 2 * HALO
        s = ext_ref[pl.ds(r0, span), cs]
        width = 1
        while width < w:
            s = s + pltpu.roll(s, span - width, axis=0)
            width *= 2
        sum_ref[:, cs] = s
        cnt = (jnp.minimum(p + w // 2, seq_len) - jnp.maximum(p - w // 2, 0)).astype(F32)
        m = sum_ref[pl.ds(HALO - w // 2, n), cs] / cnt - ext_ref[pl.ds(r0 + HALO, n), cs]
        y = jnp.dot(m.astype(BF16), wp_ref[g], preferred_element_type=F32) * ps_ref[:, cs]
        outs.append(y.astype(BF16))
    return jnp.concatenate(outs, axis=1)


def _mix_body(u_ref, up_ref, un_ref, um_ref, ya_ref, yam_ref, h_ref, hm_ref,
              wp_ref, ps_ref, wo_ref, g_ref, b_ref, h1_ref, h1m_ref,
              ext_ref, sum_ref, extm_ref, summ_ref, cat_ref, *, tile, seq_len):
    i = pl.program_id(1)
    last = pl.num_programs(1) - 1
    um = um_ref[0]
    prev = jnp.where(i == 0, um[N_META - HALO:], up_ref[0])
    nxt = jnp.where(i == last, jnp.zeros((HALO, D_POOL), F32), un_ref[0])
    ext_ref[...] = jnp.concatenate([prev, u_ref[0], nxt], axis=0)
    sub = tile // MIX_SUBTILES
    for r in range(MIX_SUBTILES):
        rows = slice(r * sub, (r + 1) * sub)
        cat_ref[rows, :D_POOL] = _pool_mixer(
            ext_ref, sum_ref.at[r], r * sub, sub, N_META + i * tile + r * sub, seq_len,
            wp_ref, ps_ref)
        cat_ref[rows, D_POOL:] = ya_ref[0, rows, :]
    for r in range(MIX_SUBTILES):
        rows = slice(r * sub, (r + 1) * sub)
        mix = jnp.dot(cat_ref[rows, :], wo_ref[...], preferred_element_type=F32)
        h1_ref[0, rows, :] = _layernorm(ALPHA * h_ref[0, rows, :] + mix, g_ref[...], b_ref[...])

    @pl.when(i == 0)
    def _():
        extm_ref[...] = jnp.concatenate(
            [jnp.zeros((HALO, D_POOL), F32), um, u_ref[0, :HALO, :]], axis=0)
        y_pool_m = _pool_mixer(extm_ref, summ_ref, 0, N_META, 0, seq_len, wp_ref, ps_ref)
        cat_m = jnp.concatenate([y_pool_m, yam_ref[0]], axis=1)
        mix_m = jnp.dot(cat_m, wo_ref[...], preferred_element_type=F32)
        h1m_ref[0] = _layernorm(ALPHA * hm_ref[0] + mix_m, g_ref[...], b_ref[...])


def _halo_specs(tile, n_tokens, width):
    per_tile = tile // HALO
    n_blocks = n_tokens // HALO
    prev = pl.BlockSpec((1, HALO, width),
                        lambda b, i, *_: (b, jnp.maximum(i * per_tile - 1, 0), 0))
    nxt = pl.BlockSpec((1, HALO, width),
                       lambda b, i, *_: (b, jnp.minimum((i + 1) * per_tile, n_blocks - 1), 0))
    return prev, nxt


def _mix(u, um, ya, yam, h, hm, w_pool_bf16, pool_scale, w_out_bf16, ln_g, ln_b, tile):
    B, T, D = h.shape
    prev_spec, next_spec = _halo_specs(tile, T, D_POOL)
    meta = lambda width: pl.BlockSpec((1, N_META, width), lambda b, i: (0, 0, 0))
    meta_b = lambda width: pl.BlockSpec((1, N_META, width), lambda b, i: (b, 0, 0))
    return pl.pallas_call(
        functools.partial(_mix_body, tile=tile, seq_len=N_META + T),
        grid=(B, T // tile),
        in_specs=[
            pl.BlockSpec((1, tile, D_POOL), lambda b, i: (b, i, 0)),
            prev_spec, next_spec,
            meta(D_POOL),
            pl.BlockSpec((1, tile, D_ATTN), lambda b, i: (b, i, 0)),
            meta_b(D_ATTN),
            pl.BlockSpec((1, tile, D), lambda b, i: (b, i, 0)),
            meta(D),
            _const_spec(w_pool_bf16.shape),
            _const_spec((1, D_POOL)),
            _const_spec((D, D)),
            _const_spec((1, D)),
            _const_spec((1, D)),
        ],
        out_specs=[
            pl.BlockSpec((1, tile, D), lambda b, i: (b, i, 0)),
            meta_b(D),
        ],
        out_shape=[
            jax.ShapeDtypeStruct((B, T, D), F32),
            jax.ShapeDtypeStruct((B, N_META, D), F32),
        ],
        scratch_shapes=[
            pltpu.VMEM((tile + 2 * HALO, D_POOL), F32),
            pltpu.VMEM((MIX_SUBTILES, tile // MIX_SUBTILES + 2 * HALO, D_POOL), F32),
            pltpu.VMEM((N_META + 2 * HALO, D_POOL), F32),
            pltpu.VMEM((N_META + 2 * HALO, D_POOL), F32),
            pltpu.VMEM((tile, D), BF16),
        ],
        compiler_params=_params(2),
        name="mix",
    )(u, u, u, um, ya, yam, h, hm, w_pool_bf16, pool_scale, w_out_bf16, ln_g, ln_b)


def _ffn_body(h1_ref, hp_ref, hn_ref, h1m_ref, wa_ref, wg_ref, ba_ref, bg_ref,
              cw_ref, cb_ref, wd_ref, g_ref, b_ref, o_ref, lhs_ref, z_ref, acc_ref, *, tile):
    i = pl.program_id(1)
    c = pl.program_id(2)
    last_tile = pl.num_programs(1) - 1
    last_chunk = pl.num_programs(2) - 1
    n_sub = FF_CHUNK // FF_SUB

    @pl.when(c == 0)
    def _():
        prev = jnp.where(i == 0, h1m_ref[0, N_META - HALO:, :], hp_ref[0])
        lhs_ref[...] = jnp.concatenate([prev, h1_ref[0], hn_ref[0]], axis=0).astype(BF16)
        acc_ref[...] = jnp.zeros_like(acc_ref)

    is_seq_end = i == last_tile
    sqrt_half = np.sqrt(0.5).astype(np.float32)

    def conv(half, s, bias_ref, row0, n_rows):
        cs = slice(s * FF_SUB, (s + 1) * FF_SUB)
        z = z_ref.at[half * n_sub + s]
        taps = [cw_ref[half, j:j + 1, cs] for j in range(3)]
        const = (taps[0] + taps[1] + taps[2]) * bias_ref[:, cs] + cb_ref[half, :, cs]
        return (z[pl.ds(HALO - 1 + row0, n_rows), :] * taps[0]
                + z[pl.ds(HALO + row0, n_rows), :] * taps[1]
                + z[pl.ds(HALO + 1 + row0, n_rows), :] * taps[2] + const)

    def gated(s, row0, n_rows):
        a = conv(0, s, ba_ref, row0, n_rows)
        g = conv(1, s, bg_ref, row0, n_rows)
        return (a * (0.5 * g * (1.0 + lax.erf(g * sqrt_half)))).astype(BF16)

    lhs = lhs_ref[...]
    for s in range(n_sub):
        cs = slice(s * FF_SUB, (s + 1) * FF_SUB)
        for half, (w_ref, bias_ref) in enumerate(((wa_ref, ba_ref), (wg_ref, bg_ref))):
            z = jnp.dot(lhs, w_ref[:, cs], preferred_element_type=F32)
            z_ref[half * n_sub + s] = z
            z_ref[half * n_sub + s, HALO + tile:, :] = jnp.where(
                is_seq_end, -bias_ref[:, cs], z[HALO + tile:, :])

    down = None
    for s in range(n_sub - 1):
        part = jnp.dot(gated(s, 0, tile), wd_ref[s * FF_SUB:(s + 1) * FF_SUB, :],
                       preferred_element_type=F32)
        down = part if down is None else down + part
    half_rows = tile // 2
    wd_last = wd_ref[(n_sub - 1) * FF_SUB:, :]
    for r in range(2):
        rows = slice(r * half_rows, (r + 1) * half_rows)
        part = jnp.dot(gated(n_sub - 1, r * half_rows, half_rows), wd_last,
                       preferred_element_type=F32)
        acc_ref[rows, :] += part if down is None else down[rows] + part

    @pl.when(c == last_chunk)
    def _():
        o_ref[0] = _layernorm(ALPHA * h1_ref[0] + acc_ref[...], g_ref[...], b_ref[...])


def _ffn(h1, h1m, w_up_p, b_up_p, conv_w_p, conv_b_p, w_down_p, ln_g, ln_b, tile):
    B, T, D = h1.shape
    prev_spec, next_spec = _halo_specs(tile, T, D)
    col_a = lambda rows: pl.BlockSpec((rows, FF_CHUNK), lambda b, i, c: (0, c))
    col_g = lambda rows: pl.BlockSpec((rows, FF_CHUNK), lambda b, i, c: (0, N_FF_CHUNKS + c))
    const3 = lambda shape: pl.BlockSpec(shape, lambda b, i, c: (0,) * len(shape),
                                        pipeline_mode=pl.Buffered(1))
    return pl.pallas_call(
        functools.partial(_ffn_body, tile=tile),
        grid=(B, T // tile, N_FF_CHUNKS),
        in_specs=[
            pl.BlockSpec((1, tile, D), lambda b, i, c: (b, i, 0)),
            prev_spec, next_spec,
            pl.BlockSpec((1, N_META, D), lambda b, i, c: (b, 0, 0)),
            col_a(D), col_g(D),
            col_a(1), col_g(1),
            pl.BlockSpec((2, 3, FF_CHUNK), lambda b, i, c: (0, 0, c)),
            pl.BlockSpec((2, 1, FF_CHUNK), lambda b, i, c: (0, 0, c)),
            pl.BlockSpec((FF_CHUNK, D), lambda b, i, c: (c, 0)),
            const3((1, D)), const3((1, D)),
        ],
        out_specs=pl.BlockSpec((1, tile, D), lambda b, i, c: (b, i, 0)),
        out_shape=jax.ShapeDtypeStruct((B, T, D), F32),
        scratch_shapes=[
            pltpu.VMEM((tile + 2 * HALO, D), BF16),
            pltpu.VMEM((2 * FF_CHUNK // FF_SUB, tile + 2 * HALO, FF_SUB), F32),
            pltpu.VMEM((tile, D), F32),
        ],
        compiler_params=_params(3),
        name="ffn",
    )(h1, h1, h1, h1m, w_up_p, w_up_p, b_up_p, b_up_p, conv_w_p, conv_b_p, w_down_p, ln_g, ln_b)


def _pad_ff_cols(a):
    zeros = jnp.zeros(a.shape[:-1] + (D_FF_PAD - D_FF,), a.dtype)
    return jnp.concatenate([a[..., :D_FF], zeros, a[..., D_FF:], zeros], axis=-1)


def _split_halves(a):
    return jnp.transpose(a.reshape(a.shape[0], 2, D_FF_PAD), (1, 0, 2))


def _encode(x, meta, consts):
    (ln_in_g, ln_in_b, w_in, w_pool, pool_scale, bias_tab, w_out, ln1_g, ln1_b,
     w_up, b_up, conv_w, conv_b, w_down, ln2_g, ln2_b) = consts
    hm, um, qm, km, vm = meta
    h, u, q, k, v = _inproj(x, ln_in_g, ln_in_b, w_in, TOKEN_TILE)
    ya, yam = _attention(q, k, v, qm, km, vm, bias_tab)
    h1, h1m = _mix(u, um, ya, yam, h, hm, w_pool, pool_scale, w_out, ln1_g, ln1_b, TOKEN_TILE)
    return _ffn(h1, h1m, w_up, b_up, conv_w, conv_b, w_down, ln2_g, ln2_b, TOKEN_TILE)


def kernel(x_prompt, x_sample, meta_tokens, ln_in_g, ln_in_b, w_in, w_pool, pool_scale, rpb,
           meta_bias, w_out, ln1_g, ln1_b, w_up, b_up, conv_w, conv_b, w_down, ln2_g, ln2_b):
    row = lambda a: a.reshape(1, -1).astype(F32)
    bias_tab = _bias_table(rpb[0], meta_bias[0])
    w_down_p = jnp.pad(w_down[0].astype(BF16), ((0, D_FF_PAD - D_FF), (0, 0)))
    consts = (
        row(ln_in_g), row(ln_in_b), w_in[0].astype(BF16), w_pool[0].astype(BF16),
        row(pool_scale[0]), bias_tab, w_out[0].astype(BF16), row(ln1_g[0]), row(ln1_b[0]),
        _pad_ff_cols(w_up[0].astype(BF16)), _pad_ff_cols(row(b_up[0])),
        _split_halves(_pad_ff_cols(conv_w[0].astype(F32))),
        _split_halves(_pad_ff_cols(row(conv_b[0]))), w_down_p,
        row(ln2_g[0]), row(ln2_b[0]),
    )
    meta = _inproj(meta_tokens[None].astype(F32), consts[0], consts[1], consts[2], N_META)
    y_prompt = _encode(x_prompt, meta, consts)
    y_sample = _encode(x_sample, meta, consts)
    return (y_prompt, y_sample)
```

```python
import functools

import numpy as np
import jax
import jax.numpy as jnp
from jax import lax
from jax.experimental import pallas as pl
from jax.experimental.pallas import tpu as pltpu

F32 = jnp.float32
BF16 = jnp.bfloat16

D_MODEL = 2048
N_META = 16
GRID_W = 64
KH = 8
KW = 16
D_POOL = D_MODEL // 2
D_ATTN = D_MODEL - D_POOL
POOL_WINDOWS = (2, 4, 8, 16)
POOL_GROUP_DIM = D_POOL // len(POOL_WINDOWS)
HEAD_DIM = 64
N_HEADS = D_ATTN // HEAD_DIM
D_FF = 5504
LN_EPS = 1e-5
DEPTH = 1
ALPHA = float((2 * DEPTH) ** 0.25)
QK_SCALE = HEAD_DIM ** -0.5
LOG2_E = float(np.log2(np.e))
MASK_VALUE = -1e30

LANES = 128
SUBLANES = 8
PAIR = 2 * HEAD_DIM
HEAD_GROUP = 256
N_HEAD_GROUPS = D_ATTN // HEAD_GROUP
PAIRS_PER_GROUP = HEAD_GROUP // PAIR
HALO = SUBLANES
VMEM_LIMIT_BYTES = 56 * 1024 * 1024

TOKEN_TILE = 512
MIX_SUBTILES = 2
ROWS_PER_STEP = 16
ROWS_PER_BLOCK = 4
KEY_ROWS = ROWS_PER_BLOCK + KH - 1
ATTN_KEYS = (KEY_ROWS + 1) * GRID_W
BLOCK_Q = ROWS_PER_BLOCK * GRID_W
N_VARIANTS = 3
FF_CHUNK = 512
D_FF_PAD = -(-D_FF // FF_CHUNK) * FF_CHUNK
N_FF_CHUNKS = D_FF_PAD // FF_CHUNK
FF_SUB = 256


def _layernorm(x, g, b):
    mu = jnp.mean(x, axis=-1, keepdims=True)
    xc = x - mu
    var = jnp.mean(xc * xc, axis=-1, keepdims=True)
    return xc * lax.rsqrt(var + LN_EPS) * g + b


def _const_spec(shape):
    zeros = (0,) * len(shape)
    return pl.BlockSpec(shape, lambda *_: zeros, pipeline_mode=pl.Buffered(1))


def _params(n_axes, flags=None):
    return pltpu.CompilerParams(
        dimension_semantics=("arbitrary",) * n_axes,
        vmem_limit_bytes=VMEM_LIMIT_BYTES,
        flags=flags,
    )


def _inproj_body(x_ref, g_ref, b_ref, w_ref, h_ref, u_ref, q_ref, k_ref, v_ref):
    h = _layernorm(x_ref[0], g_ref[...], b_ref[...])
    h_ref[0] = h
    hb = h.astype(BF16)
    u_ref[0] = jnp.dot(hb, w_ref[:, 0:D_POOL], preferred_element_type=F32)
    for j in range(N_HEAD_GROUPS):
        lo = D_POOL + j * HEAD_GROUP
        q = jnp.dot(hb, w_ref[:, lo:lo + HEAD_GROUP], preferred_element_type=F32)
        q_ref[0, j] = (q * (QK_SCALE * LOG2_E)).astype(BF16)
        lo += D_ATTN
        k_ref[0, j] = jnp.dot(hb, w_ref[:, lo:lo + HEAD_GROUP],
                              preferred_element_type=F32).astype(BF16)
        lo += D_ATTN
        v_ref[0, j] = jnp.dot(hb, w_ref[:, lo:lo + HEAD_GROUP],
                              preferred_element_type=F32).astype(BF16)


def _inproj(x, ln_g, ln_b, w_in_bf16, tile):
    B, T, D = x.shape
    n_in = w_in_bf16.shape[1]
    hg_shape = (B, N_HEAD_GROUPS, T, HEAD_GROUP)
    hg_spec = pl.BlockSpec((1, N_HEAD_GROUPS, tile, HEAD_GROUP), lambda b, i: (b, 0, i, 0))
    return pl.pallas_call(
        _inproj_body,
        grid=(B, T // tile),
        in_specs=[
            pl.BlockSpec((1, tile, D), lambda b, i: (b, i, 0)),
            _const_spec((1, D)),
            _const_spec((1, D)),
            _const_spec((D, n_in)),
        ],
        out_specs=[
            pl.BlockSpec((1, tile, D), lambda b, i: (b, i, 0)),
            pl.BlockSpec((1, tile, D_POOL), lambda b, i: (b, i, 0)),
            hg_spec, hg_spec, hg_spec,
        ],
        out_shape=[
            jax.ShapeDtypeStruct((B, T, D), F32),
            jax.ShapeDtypeStruct((B, T, D_POOL), F32),
            jax.ShapeDtypeStruct(hg_shape, BF16),
            jax.ShapeDtypeStruct(hg_shape, BF16),
            jax.ShapeDtypeStruct(hg_shape, BF16),
        ],
        compiler_params=_params(2),
        name="inproj",
    )(x, ln_g, ln_b, w_in_bf16)


def _attend(q2, keys, vals, bias):
    m = q2.shape[0]
    first = lax.broadcasted_iota(jnp.int32, (m, PAIR), 1) < HEAD_DIM
    zero = jnp.zeros_like(q2)
    qs = jnp.concatenate([jnp.where(first, q2, zero), jnp.where(first, zero, q2)], axis=0)
    s = lax.dot_general(qs, keys, (((1,), (1,)), ((), ())), preferred_element_type=F32) + bias
    e = jnp.exp2(s - jnp.max(s, axis=-1, keepdims=True)).astype(BF16)
    vals_ones = jnp.concatenate([vals, jnp.ones_like(vals)], axis=1)
    o = jnp.dot(e, vals_ones, preferred_element_type=F32)
    o = o[:, :PAIR] / o[:, PAIR:]
    return jnp.where(first, o[:m], o[m:])


def _attn_body(q_ref, k_ref, v_ref, qm_ref, km_ref, vm_ref, bias_ref, y_ref, ym_ref, *, rows):
    rb = pl.program_id(2)
    n_real = KEY_ROWS * GRID_W
    pad = jnp.zeros((GRID_W - N_META, PAIR), BF16)

    def key_set(ref, meta_ref, k_off, ls):
        return jnp.concatenate(
            [ref[0, 0, pl.ds(k_off, n_real), ls], meta_ref[0, 0, :, ls], pad], axis=0)

    for blk in range(ROWS_PER_STEP // ROWS_PER_BLOCK):
        r0 = rb * ROWS_PER_STEP + blk * ROWS_PER_BLOCK
        ks = jnp.clip(r0 - KH // 2, 0, rows - KEY_ROWS)
        variant = jnp.where(r0 == 0, 0, jnp.where(r0 == rows - ROWS_PER_BLOCK, 2, 1))
        k_off = pl.multiple_of(ks * GRID_W, GRID_W)
        qs = slice(blk * BLOCK_Q, (blk + 1) * BLOCK_Q)
        for p in range(PAIRS_PER_GROUP):
            ls = slice(p * PAIR, (p + 1) * PAIR)
            out = _attend(q_ref[0, 0, qs, ls], key_set(k_ref, km_ref, k_off, ls),
                          key_set(v_ref, vm_ref, k_off, ls), bias_ref[variant, p])
            y_ref[0, qs, ls] = out.astype(BF16)

    @pl.when(rb == 0)
    def _():
        for p in range(PAIRS_PER_GROUP):
            ls = slice(p * PAIR, (p + 1) * PAIR)
            bias = jnp.concatenate(
                [jnp.broadcast_to(bias_ref[0, p, 0:1, :], (N_META, ATTN_KEYS)),
                 jnp.broadcast_to(bias_ref[0, p, BLOCK_Q:BLOCK_Q + 1, :], (N_META, ATTN_KEYS))],
                axis=0)
            out = _attend(qm_ref[0, 0, :, ls], key_set(k_ref, km_ref, 0, ls),
                          key_set(v_ref, vm_ref, 0, ls), bias)
            ym_ref[0, :, ls] = out.astype(BF16)


def _attention(q, k, v, qm, km, vm, bias_tab):
    B, _, T, _ = q.shape
    rows = T // GRID_W
    assert rows % ROWS_PER_STEP == 0 and rows >= KEY_ROWS + 1
    q_rows = ROWS_PER_STEP * GRID_W
    meta_spec = pl.BlockSpec((1, 1, N_META, HEAD_GROUP), lambda b, g, i: (0, g, 0, 0))
    kv_spec = pl.BlockSpec((1, 1, T, HEAD_GROUP), lambda b, g, i: (b, g, 0, 0))
    return pl.pallas_call(
        functools.partial(_attn_body, rows=rows),
        grid=(B, N_HEAD_GROUPS, rows // ROWS_PER_STEP),
        in_specs=[
            pl.BlockSpec((1, 1, q_rows, HEAD_GROUP), lambda b, g, i: (b, g, i, 0)),
            kv_spec, kv_spec,
            meta_spec, meta_spec, meta_spec,
            pl.BlockSpec((N_VARIANTS, PAIRS_PER_GROUP, 2 * BLOCK_Q, ATTN_KEYS),
                         lambda b, g, i: (0, g, 0, 0)),
        ],
        out_specs=[
            pl.BlockSpec((1, q_rows, HEAD_GROUP), lambda b, g, i: (b, i, g)),
            pl.BlockSpec((1, N_META, HEAD_GROUP), lambda b, g, i: (b, 0, g)),
        ],
        out_shape=[
            jax.ShapeDtypeStruct((B, T, D_ATTN), BF16),
            jax.ShapeDtypeStruct((B, N_META, D_ATTN), BF16),
        ],
        compiler_params=_params(3),
        name="attention",
    )(q, k, v, qm, km, vm, bias_tab)


def _bias_table(rpb, meta_bias):
    c = np.arange(GRID_W)
    kc = np.arange(GRID_W)
    cs = np.clip(c - KW // 2, 0, GRID_W - KW)
    valid = (kc[None, :] >= cs[:, None]) & (kc[None, :] < cs[:, None] + KW)
    dc = kc[None, :] - c[:, None] + (KW - 1)
    onehot = ((dc[None] == np.arange(2 * KW - 1)[:, None, None]) & valid[None]).astype(np.float32)
    toep = jnp.einsum('hrj,jck->hrck', rpb.astype(F32) * LOG2_E, onehot,
                      precision=lax.Precision.HIGHEST)
    toep = jnp.where(valid[None, None], toep, MASK_VALUE)
    toep = jnp.pad(toep, ((0, 0), (1, 1), (0, 0), (0, 0)), constant_values=MASK_VALUE)
    pairs = jnp.concatenate([toep[:, :-1], toep[:, 1:]], axis=-1)
    meta = jnp.pad(meta_bias.astype(F32) * LOG2_E, ((0, 0), (GRID_W, GRID_W - N_META)),
                   constant_values=MASK_VALUE).reshape(N_HEADS, 1, PAIR)
    heads_per_step = 4
    tab = pl.pallas_call(
        functools.partial(_bias_body, heads=heads_per_step),
        grid=(N_VARIANTS, N_HEADS // heads_per_step),
        in_specs=[
            pl.BlockSpec((heads_per_step, 2 * KH, GRID_W, PAIR), lambda v, g: (g, 0, 0, 0)),
            pl.BlockSpec((heads_per_step, 1, PAIR), lambda v, g: (g, 0, 0)),
        ],
        out_specs=pl.BlockSpec((1, heads_per_step, BLOCK_Q, ATTN_KEYS), lambda v, g: (v, g, 0, 0)),
        out_shape=jax.ShapeDtypeStruct((N_VARIANTS, N_HEADS, BLOCK_Q, ATTN_KEYS), F32),
        compiler_params=_params(2),
        name="bias_table",
    )(pairs, meta)
    return tab.reshape(N_VARIANTS, N_HEADS // 2, 2 * BLOCK_Q, ATTN_KEYS)


def _bias_body(pairs_ref, meta_ref, out_ref, *, heads):
    v = pl.program_id(0)
    offset = jnp.where(v == 0, 0, jnp.where(v == 1, KH // 2, KEY_ROWS - ROWS_PER_BLOCK))
    left = lax.broadcasted_iota(jnp.int32, (GRID_W, PAIR), 1) < GRID_W
    for h in range(heads):
        meta_tile = jnp.broadcast_to(meta_ref[h], (GRID_W, PAIR))
        for ri in range(ROWS_PER_BLOCK):
            lo = jnp.where(v == 0, 0, jnp.where(v == 1, ri, KH // 2 - 1))
            for j in range(ATTN_KEYS // PAIR):
                kr = 2 * j
                dr = kr - offset - ri + (KH - 1)
                tile = pairs_ref[h, jnp.clip(dr + 1, 0, 2 * KH - 1)]
                in_left = jnp.logical_and(kr >= lo, kr < lo + KH)
                in_right = jnp.logical_and(kr + 1 >= lo, kr + 1 < lo + KH)
                if kr + 1 < KEY_ROWS:
                    right_tile = jnp.where(in_right, tile, MASK_VALUE)
                else:
                    right_tile = meta_tile
                tile = jnp.where(left, jnp.where(in_left, tile, MASK_VALUE), right_tile)
                out_ref[0, h, ri * GRID_W:(ri + 1) * GRID_W, j * PAIR:(j + 1) * PAIR] = tile


def _pool_mixer(ext_ref, sum_ref, r0, n, p0, seq_len, wp_ref, ps_ref):
    p = p0 + lax.broadcasted_iota(jnp.int32, (n, 1), 0)
    outs = []
    for g, w in enumerate(POOL_WINDOWS):
        cs = slice(g * POOL_GROUP_DIM, (g + 1) * POOL_GROUP_DIM)
        span = n + 2 * HALO
        s = ext_ref[pl.ds(r0, span), cs]
        width = 1
        while width < w:
            s = s + pltpu.roll(s, span - width, axis=0)
            width *= 2
        sum_ref[:, cs] = s
        cnt = (jnp.minimum(p + w // 2, seq_len) - jnp.maximum(p - w // 2, 0)).astype(F32)
        m = sum_ref[pl.ds(HALO - w // 2, n), cs] / cnt - ext_ref[pl.ds(r0 + HALO, n), cs]
        y = jnp.dot(m.astype(BF16), wp_ref[g], preferred_element_type=F32) * ps_ref[:, cs]
        outs.append(y.astype(BF16))
    return jnp.concatenate(outs, axis=1)


def _mix_body(u_ref, up_ref, un_ref, um_ref, ya_ref, yam_ref, h_ref, hm_ref,
              wp_ref, ps_ref, wo_ref, g_ref, b_ref, h1_ref, h1m_ref,
              ext_ref, sum_ref, extm_ref, summ_ref, cat_ref, *, tile, seq_len):
    i = pl.program_id(1)
    last = pl.num_programs(1) - 1
    um = um_ref[0]
    prev = jnp.where(i == 0, um[N_META - HALO:], up_ref[0])
    nxt = jnp.where(i == last, jnp.zeros((HALO, D_POOL), F32), un_ref[0])
    ext_ref[...] = jnp.concatenate([prev, u_ref[0], nxt], axis=0)
    sub = tile // MIX_SUBTILES
    for r in range(MIX_SUBTILES):
        rows = slice(r * sub, (r + 1) * sub)
        cat_ref[rows, :D_POOL] = _pool_mixer(
            ext_ref, sum_ref.at[r], r * sub, sub, N_META + i * tile + r * sub, seq_len,
            wp_ref, ps_ref)
        cat_ref[rows, D_POOL:] = ya_ref[0, rows, :]
    for r in range(MIX_SUBTILES):
        rows = slice(r * sub, (r + 1) * sub)
        mix = jnp.dot(cat_ref[rows, :], wo_ref[...], preferred_element_type=F32)
        h1_ref[0, rows, :] = _layernorm(ALPHA * h_ref[0, rows, :] + mix, g_ref[...], b_ref[...])

    @pl.when(i == 0)
    def _():
        extm_ref[...] = jnp.concatenate(
            [jnp.zeros((HALO, D_POOL), F32), um, u_ref[0, :HALO, :]], axis=0)
        y_pool_m = _pool_mixer(extm_ref, summ_ref, 0, N_META, 0, seq_len, wp_ref, ps_ref)
        cat_m = jnp.concatenate([y_pool_m, yam_ref[0]], axis=1)
        mix_m = jnp.dot(cat_m, wo_ref[...], preferred_element_type=F32)
        h1m_ref[0] = _layernorm(ALPHA * hm_ref[0] + mix_m, g_ref[...], b_ref[...])


def _halo_specs(tile, n_tokens, width):
    per_tile = tile // HALO
    n_blocks = n_tokens // HALO
    prev = pl.BlockSpec((1, HALO, width),
                        lambda b, i, *_: (b, jnp.maximum(i * per_tile - 1, 0), 0))
    nxt = pl.BlockSpec((1, HALO, width),
                       lambda b, i, *_: (b, jnp.minimum((i + 1) * per_tile, n_blocks - 1), 0))
    return prev, nxt


def _mix(u, um, ya, yam, h, hm, w_pool_bf16, pool_scale, w_out_bf16, ln_g, ln_b, tile):
    B, T, D = h.shape
    prev_spec, next_spec = _halo_specs(tile, T, D_POOL)
    meta = lambda width: pl.BlockSpec((1, N_META, width), lambda b, i: (0, 0, 0))
    meta_b = lambda width: pl.BlockSpec((1, N_META, width), lambda b, i: (b, 0, 0))
    return pl.pallas_call(
        functools.partial(_mix_body, tile=tile, seq_len=N_META + T),
        grid=(B, T // tile),
        in_specs=[
            pl.BlockSpec((1, tile, D_POOL), lambda b, i: (b, i, 0)),
            prev_spec, next_spec,
            meta(D_POOL),
            pl.BlockSpec((1, tile, D_ATTN), lambda b, i: (b, i, 0)),
            meta_b(D_ATTN),
            pl.BlockSpec((1, tile, D), lambda b, i: (b, i, 0)),
            meta(D),
            _const_spec(w_pool_bf16.shape),
            _const_spec((1, D_POOL)),
            _const_spec((D, D)),
            _const_spec((1, D)),
            _const_spec((1, D)),
        ],
        out_specs=[
            pl.BlockSpec((1, tile, D), lambda b, i: (b, i, 0)),
            meta_b(D),
        ],
        out_shape=[
            jax.ShapeDtypeStruct((B, T, D), F32),
            jax.ShapeDtypeStruct((B, N_META, D), F32),
        ],
        scratch_shapes=[
            pltpu.VMEM((tile + 2 * HALO, D_POOL), F32),
            pltpu.VMEM((MIX_SUBTILES, tile // MIX_SUBTILES + 2 * HALO, D_POOL), F32),
            pltpu.VMEM((N_META + 2 * HALO, D_POOL), F32),
            pltpu.VMEM((N_META + 2 * HALO, D_POOL), F32),
            pltpu.VMEM((tile, D), BF16),
        ],
        compiler_params=_params(2),
        name="mix",
    )(u, u, u, um, ya, yam, h, hm, w_pool_bf16, pool_scale, w_out_bf16, ln_g, ln_b)


def _ffn_body(h1_ref, hp_ref, hn_ref, h1m_ref, wa_ref, wg_ref, ba_ref, bg_ref,
              cw_ref, cb_ref, wd_ref, g_ref, b_ref, o_ref, lhs_ref, z_ref, acc_ref, *, tile):
    i = pl.program_id(1)
    c = pl.program_id(2)
    last_tile = pl.num_programs(1) - 1
    last_chunk = pl.num_programs(2) - 1
    n_sub = FF_CHUNK // FF_SUB

    @pl.when(c == 0)
    def _():
        prev = jnp.where(i == 0, h1m_ref[0, N_META - HALO:, :], hp_ref[0])
        lhs_ref[...] = jnp.concatenate([prev, h1_ref[0], hn_ref[0]], axis=0).astype(BF16)
        acc_ref[...] = jnp.zeros_like(acc_ref)

    is_seq_end = i == last_tile
    sqrt_half = np.sqrt(0.5).astype(np.float32)

    def conv(half, s, bias_ref, row0, n_rows):
        cs = slice(s * FF_SUB, (s + 1) * FF_SUB)
        z = z_ref.at[half * n_sub + s]
        taps = [cw_ref[half, j:j + 1, cs] for j in range(3)]
        const = (taps[0] + taps[1] + taps[2]) * bias_ref[:, cs] + cb_ref[half, :, cs]
        return (z[pl.ds(HALO - 1 + row0, n_rows), :] * taps[0]
                + z[pl.ds(HALO + row0, n_rows), :] * taps[1]
                + z[pl.ds(HALO + 1 + row0, n_rows), :] * taps[2] + const)

    def gated(s, row0, n_rows):
        a = conv(0, s, ba_ref, row0, n_rows)
        g = conv(1, s, bg_ref, row0, n_rows)
        return (a * (0.5 * g * (1.0 + lax.erf(g * sqrt_half)))).astype(BF16)

    lhs = lhs_ref[...]
    for s in range(n_sub):
        cs = slice(s * FF_SUB, (s + 1) * FF_SUB)
        for half, (w_ref, bias_ref) in enumerate(((wa_ref, ba_ref), (wg_ref, bg_ref))):
            z = jnp.dot(lhs, w_ref[:, cs], preferred_element_type=F32)
            z_ref[half * n_sub + s] = z
            z_ref[half * n_sub + s, HALO + tile:, :] = jnp.where(
                is_seq_end, -bias_ref[:, cs], z[HALO + tile:, :])

    down = None
    for s in range(n_sub - 1):
        part = jnp.dot(gated(s, 0, tile), wd_ref[s * FF_SUB:(s + 1) * FF_SUB, :],
                       preferred_element_type=F32)
        down = part if down is None else down + part
    half_rows = tile // 2
    wd_last = wd_ref[(n_sub - 1) * FF_SUB:, :]
    for r in range(2):
        rows = slice(r * half_rows, (r + 1) * half_rows)
        part = jnp.dot(gated(n_sub - 1, r * half_rows, half_rows), wd_last,
                       preferred_element_type=F32)
        acc_ref[rows, :] += part if down is None else down[rows] + part

    @pl.when(c == last_chunk)
    def _():
        o_ref[0] = _layernorm(ALPHA * h1_ref[0] + acc_ref[...], g_ref[...], b_ref[...])


def _ffn(h1, h1m, w_up_p, b_up_p, conv_w_p, conv_b_p, w_down_p, ln_g, ln_b, tile):
    B, T, D = h1.shape
    prev_spec, next_spec = _halo_specs(tile, T, D)
    col_a = lambda rows: pl.BlockSpec((rows, FF_CHUNK), lambda b, i, c: (0, c))
    col_g = lambda rows: pl.BlockSpec((rows, FF_CHUNK), lambda b, i, c: (0, N_FF_CHUNKS + c))
    const3 = lambda shape: pl.BlockSpec(shape, lambda b, i, c: (0,) * len(shape),
                                        pipeline_mode=pl.Buffered(1))
    return pl.pallas_call(
        functools.partial(_ffn_body, tile=tile),
        grid=(B, T // tile, N_FF_CHUNKS),
        in_specs=[
            pl.BlockSpec((1, tile, D), lambda b, i, c: (b, i, 0)),
            prev_spec, next_spec,
            pl.BlockSpec((1, N_META, D), lambda b, i, c: (b, 0, 0)),
            col_a(D), col_g(D),
            col_a(1), col_g(1),
            pl.BlockSpec((2, 3, FF_CHUNK), lambda b, i, c: (0, 0, c)),
            pl.BlockSpec((2, 1, FF_CHUNK), lambda b, i, c: (0, 0, c)),
            pl.BlockSpec((FF_CHUNK, D), lambda b, i, c: (c, 0)),
            const3((1, D)), const3((1, D)),
        ],
        out_specs=pl.BlockSpec((1, tile, D), lambda b, i, c: (b, i, 0)),
        out_shape=jax.ShapeDtypeStruct((B, T, D), F32),
        scratch_shapes=[
            pltpu.VMEM((tile + 2 * HALO, D), BF16),
            pltpu.VMEM((2 * FF_CHUNK // FF_SUB, tile + 2 * HALO, FF_SUB), F32),
            pltpu.VMEM((tile, D), F32),
        ],
        compiler_params=_params(3),
        name="ffn",
    )(h1, h1, h1, h1m, w_up_p, w_up_p, b_up_p, b_up_p, conv_w_p, conv_b_p, w_down_p, ln_g, ln_b)


def _pad_ff_cols(a):
    zeros = jnp.zeros(a.shape[:-1] + (D_FF_PAD - D_FF,), a.dtype)
    return jnp.concatenate([a[..., :D_FF], zeros, a[..., D_FF:], zeros], axis=-1)


def _w_up_body(w_ref, o_ref):
    zeros = jnp.zeros((w_ref.shape[0], D_FF_PAD - D_FF), BF16)
    o_ref[:, :D_FF] = w_ref[:, :D_FF].astype(BF16)
    o_ref[:, D_FF:D_FF_PAD] = zeros
    o_ref[:, D_FF_PAD:D_FF_PAD + D_FF] = w_ref[:, D_FF:].astype(BF16)
    o_ref[:, D_FF_PAD + D_FF:] = zeros


def _prep_w_up(w_up):
    D = w_up.shape[0]
    rows = 128
    return pl.pallas_call(
        _w_up_body,
        grid=(D // rows,),
        in_specs=[pl.BlockSpec((rows, 2 * D_FF), lambda i: (i, 0))],
        out_specs=pl.BlockSpec((rows, 2 * D_FF_PAD), lambda i: (i, 0)),
        out_shape=jax.ShapeDtypeStruct((D, 2 * D_FF_PAD), BF16),
        compiler_params=_params(1),
        name="prep_w_up",
    )(w_up)


def _split_halves(a):
    return jnp.transpose(a.reshape(a.shape[0], 2, D_FF_PAD), (1, 0, 2))


def _encode(x, meta, consts):
    (ln_in_g, ln_in_b, w_in, w_pool, pool_scale, bias_tab, w_out, ln1_g, ln1_b,
     w_up, b_up, conv_w, conv_b, w_down, ln2_g, ln2_b) = consts
    hm, um, qm, km, vm = meta
    h, u, q, k, v = _inproj(x, ln_in_g, ln_in_b, w_in, TOKEN_TILE)
    ya, yam = _attention(q, k, v, qm, km, vm, bias_tab)
    h1, h1m = _mix(u, um, ya, yam, h, hm, w_pool, pool_scale, w_out, ln1_g, ln1_b, TOKEN_TILE)
    return _ffn(h1, h1m, w_up, b_up, conv_w, conv_b, w_down, ln2_g, ln2_b, TOKEN_TILE)


def kernel(x_prompt, x_sample, meta_tokens, ln_in_g, ln_in_b, w_in, w_pool, pool_scale, rpb,
           meta_bias, w_out, ln1_g, ln1_b, w_up, b_up, conv_w, conv_b, w_down, ln2_g, ln2_b):
    row = lambda a: a.reshape(1, -1).astype(F32)
    bias_tab = _bias_table(rpb[0], meta_bias[0])
    w_down_p = jnp.concatenate(
        [w_down[0].astype(BF16), jnp.zeros((D_FF_PAD - D_FF, D_MODEL), BF16)], axis=0)
    consts = (
        row(ln_in_g), row(ln_in_b), w_in[0].astype(BF16), w_pool[0].astype(BF16),
        row(pool_scale[0]), bias_tab, w_out[0].astype(BF16), row(ln1_g[0]), row(ln1_b[0]),
        _prep_w_up(w_up[0].astype(F32)), _pad_ff_cols(row(b_up[0])),
        _split_halves(_pad_ff_cols(conv_w[0].astype(F32))),
        _split_halves(_pad_ff_cols(row(conv_b[0]))), w_down_p,
        row(ln2_g[0]), row(ln2_b[0]),
    )
    meta = _inproj(meta_tokens[None].astype(F32), consts[0], consts[1], consts[2], N_META)
    y_prompt = _encode(x_prompt, meta, consts)
    y_sample = _encode(x_sample, meta, consts)
    return (y_prompt, y_sample)
```

```python
import functools

import numpy as np
import jax
import jax.numpy as jnp
from jax import lax
from jax.experimental import pallas as pl
from jax.experimental.pallas import tpu as pltpu

F32 = jnp.float32
BF16 = jnp.bfloat16

D_MODEL = 2048
N_META = 16
GRID_W = 64
KH = 8
KW = 16
D_POOL = D_MODEL // 2
D_ATTN = D_MODEL - D_POOL
POOL_WINDOWS = (2, 4, 8, 16)
POOL_GROUP_DIM = D_POOL // len(POOL_WINDOWS)
HEAD_DIM = 64
N_HEADS = D_ATTN // HEAD_DIM
D_FF = 5504
LN_EPS = 1e-5
DEPTH = 1
ALPHA = float((2 * DEPTH) ** 0.25)
QK_SCALE = HEAD_DIM ** -0.5
LOG2_E = float(np.log2(np.e))
MASK_VALUE = -1e30

LANES = 128
SUBLANES = 8
PAIR = 2 * HEAD_DIM
HEAD_GROUP = 256
N_HEAD_GROUPS = D_ATTN // HEAD_GROUP
PAIRS_PER_GROUP = HEAD_GROUP // PAIR
HALO = SUBLANES
VMEM_LIMIT_BYTES = 56 * 1024 * 1024

TOKEN_TILE = 512
MIX_SUBTILES = 2
INPROJ_SUBTILES = 2
ROWS_PER_STEP = 32
ROWS_PER_BLOCK = 4
KEY_ROWS = ROWS_PER_BLOCK + KH - 1
ATTN_KEYS = (KEY_ROWS + 1) * GRID_W
BLOCK_Q = ROWS_PER_BLOCK * GRID_W
N_VARIANTS = 3
FF_CHUNK = 512
D_FF_PAD = -(-D_FF // FF_CHUNK) * FF_CHUNK
N_FF_CHUNKS = D_FF_PAD // FF_CHUNK
FF_SUB = 256


def _layernorm(x, g, b):
    mu = jnp.mean(x, axis=-1, keepdims=True)
    xc = x - mu
    var = jnp.mean(xc * xc, axis=-1, keepdims=True)
    return xc * lax.rsqrt(var + LN_EPS) * g + b


def _const_spec(shape):
    zeros = (0,) * len(shape)
    return pl.BlockSpec(shape, lambda *_: zeros, pipeline_mode=pl.Buffered(1))


def _params(n_axes, flags=None):
    return pltpu.CompilerParams(
        dimension_semantics=("arbitrary",) * n_axes,
        vmem_limit_bytes=VMEM_LIMIT_BYTES,
        flags=flags,
    )


def _inproj_body(x_ref, g_ref, b_ref, w_ref, h_ref, u_ref, q_ref, k_ref, v_ref, hb_ref,
                 *, subtiles):
    sub = x_ref.shape[1] // subtiles

    def norm(r):
        rows = slice(r * sub, (r + 1) * sub)
        h = _layernorm(x_ref[0, rows, :], g_ref[...], b_ref[...])
        h_ref[0, rows, :] = h
        hb_ref[rows, :] = h.astype(BF16)

    def pool_proj(r):
        rows = slice(r * sub, (r + 1) * sub)
        u_ref[0, rows, :] = jnp.dot(hb_ref[rows, :], w_ref[:, 0:D_POOL],
                                    preferred_element_type=F32)

    def qkv_proj(r):
        rows = slice(r * sub, (r + 1) * sub)
        hb = hb_ref[rows, :]
        for j in range(N_HEAD_GROUPS):
            lo = D_POOL + j * HEAD_GROUP
            q = jnp.dot(hb, w_ref[:, lo:lo + HEAD_GROUP], preferred_element_type=F32)
            q_ref[0, j, rows, :] = (q * (QK_SCALE * LOG2_E)).astype(BF16)
            lo += D_ATTN
            k_ref[0, j, rows, :] = jnp.dot(hb, w_ref[:, lo:lo + HEAD_GROUP],
                                           preferred_element_type=F32).astype(BF16)
            lo += D_ATTN
            v_ref[0, j, rows, :] = jnp.dot(hb, w_ref[:, lo:lo + HEAD_GROUP],
                                           preferred_element_type=F32).astype(BF16)

    norm(0)
    for r in range(subtiles):
        pool_proj(r)
        if r + 1 < subtiles:
            norm(r + 1)
        qkv_proj(r)


def _inproj(x, ln_g, ln_b, w_in_bf16, tile, subtiles):
    B, T, D = x.shape
    n_in = w_in_bf16.shape[1]
    hg_shape = (B, N_HEAD_GROUPS, T, HEAD_GROUP)
    hg_spec = pl.BlockSpec((1, N_HEAD_GROUPS, tile, HEAD_GROUP), lambda b, i: (b, 0, i, 0))
    return pl.pallas_call(
        functools.partial(_inproj_body, subtiles=subtiles),
        grid=(B, T // tile),
        in_specs=[
            pl.BlockSpec((1, tile, D), lambda b, i: (b, i, 0)),
            _const_spec((1, D)),
            _const_spec((1, D)),
            _const_spec((D, n_in)),
        ],
        out_specs=[
            pl.BlockSpec((1, tile, D), lambda b, i: (b, i, 0)),
            pl.BlockSpec((1, tile, D_POOL), lambda b, i: (b, i, 0)),
            hg_spec, hg_spec, hg_spec,
        ],
        out_shape=[
            jax.ShapeDtypeStruct((B, T, D), F32),
            jax.ShapeDtypeStruct((B, T, D_POOL), F32),
            jax.ShapeDtypeStruct(hg_shape, BF16),
            jax.ShapeDtypeStruct(hg_shape, BF16),
            jax.ShapeDtypeStruct(hg_shape, BF16),
        ],
        scratch_shapes=[pltpu.VMEM((tile, D), BF16)],
        compiler_params=_params(2),
        name="inproj",
    )(x, ln_g, ln_b, w_in_bf16)


def _attend(q2, keys, vals, bias):
    m = q2.shape[0]
    first = lax.broadcasted_iota(jnp.int32, (m, PAIR), 1) < HEAD_DIM
    zero = jnp.zeros_like(q2)
    qs = jnp.concatenate([jnp.where(first, q2, zero), jnp.where(first, zero, q2)], axis=0)
    s = lax.dot_general(qs, keys, (((1,), (1,)), ((), ())), preferred_element_type=F32) + bias
    e = jnp.exp2(s - jnp.max(s, axis=-1, keepdims=True)).astype(BF16)
    vals_ones = jnp.concatenate([vals, jnp.ones_like(vals)], axis=1)
    o = jnp.dot(e, vals_ones, preferred_element_type=F32)
    o = o[:, :PAIR] / o[:, PAIR:]
    return jnp.where(first, o[:m], o[m:])


def _attn_body(q_ref, k_ref, v_ref, qm_ref, km_ref, vm_ref, bias_ref, y_ref, ym_ref, *, rows):
    rb = pl.program_id(2)
    n_real = KEY_ROWS * GRID_W
    pad = jnp.zeros((GRID_W - N_META, PAIR), BF16)

    def key_set(ref, meta_ref, k_off, ls):
        return jnp.concatenate(
            [ref[0, 0, pl.ds(k_off, n_real), ls], meta_ref[0, 0, :, ls], pad], axis=0)

    for blk in range(ROWS_PER_STEP // ROWS_PER_BLOCK):
        r0 = rb * ROWS_PER_STEP + blk * ROWS_PER_BLOCK
        ks = jnp.clip(r0 - KH // 2, 0, rows - KEY_ROWS)
        variant = jnp.where(r0 == 0, 0, jnp.where(r0 == rows - ROWS_PER_BLOCK, 2, 1))
        k_off = pl.multiple_of(ks * GRID_W, GRID_W)
        qs = slice(blk * BLOCK_Q, (blk + 1) * BLOCK_Q)
        for p in range(PAIRS_PER_GROUP):
            ls = slice(p * PAIR, (p + 1) * PAIR)
            out = _attend(q_ref[0, 0, qs, ls], key_set(k_ref, km_ref, k_off, ls),
                          key_set(v_ref, vm_ref, k_off, ls), bias_ref[variant, p])
            y_ref[0, qs, ls] = out.astype(BF16)

    @pl.when(rb == 0)
    def _():
        for p in range(PAIRS_PER_GROUP):
            ls = slice(p * PAIR, (p + 1) * PAIR)
            bias = jnp.concatenate(
                [jnp.broadcast_to(bias_ref[0, p, 0:1, :], (N_META, ATTN_KEYS)),
                 jnp.broadcast_to(bias_ref[0, p, BLOCK_Q:BLOCK_Q + 1, :], (N_META, ATTN_KEYS))],
                axis=0)
            out = _attend(qm_ref[0, 0, :, ls], key_set(k_ref, km_ref, 0, ls),
                          key_set(v_ref, vm_ref, 0, ls), bias)
            ym_ref[0, :, ls] = out.astype(BF16)


def _attention(q, k, v, qm, km, vm, bias_tab):
    B, _, T, _ = q.shape
    rows = T // GRID_W
    assert rows % ROWS_PER_STEP == 0 and rows >= KEY_ROWS + 1
    q_rows = ROWS_PER_STEP * GRID_W
    meta_spec = pl.BlockSpec((1, 1, N_META, HEAD_GROUP), lambda b, g, i: (0, g, 0, 0))
    kv_spec = pl.BlockSpec((1, 1, T, HEAD_GROUP), lambda b, g, i: (b, g, 0, 0))
    return pl.pallas_call(
        functools.partial(_attn_body, rows=rows),
        grid=(B, N_HEAD_GROUPS, rows // ROWS_PER_STEP),
        in_specs=[
            pl.BlockSpec((1, 1, q_rows, HEAD_GROUP), lambda b, g, i: (b, g, i, 0)),
            kv_spec, kv_spec,
            meta_spec, meta_spec, meta_spec,
            pl.BlockSpec((N_VARIANTS, PAIRS_PER_GROUP, 2 * BLOCK_Q, ATTN_KEYS),
                         lambda b, g, i: (0, g, 0, 0)),
        ],
        out_specs=[
            pl.BlockSpec((1, q_rows, HEAD_GROUP), lambda b, g, i: (b, i, g)),
            pl.BlockSpec((1, N_META, HEAD_GROUP), lambda b, g, i: (b, 0, g)),
        ],
        out_shape=[
            jax.ShapeDtypeStruct((B, T, D_ATTN), BF16),
            jax.ShapeDtypeStruct((B, N_META, D_ATTN), BF16),
        ],
        compiler_params=_params(3),
        name="attention",
    )(q, k, v, qm, km, vm, bias_tab)


def _bias_table(rpb, meta_bias):
    c = np.arange(GRID_W)
    kc = np.arange(GRID_W)
    cs = np.clip(c - KW // 2, 0, GRID_W - KW)
    valid = (kc[None, :] >= cs[:, None]) & (kc[None, :] < cs[:, None] + KW)
    dc = kc[None, :] - c[:, None] + (KW - 1)
    onehot = ((dc[None] == np.arange(2 * KW - 1)[:, None, None]) & valid[None]).astype(np.float32)
    toep = jnp.einsum('hrj,jck->hrck', rpb.astype(F32) * LOG2_E, onehot,
                      precision=lax.Precision.HIGHEST)
    toep = jnp.where(valid[None, None], toep, MASK_VALUE)
    toep = jnp.pad(toep, ((0, 0), (1, 1), (0, 0), (0, 0)), constant_values=MASK_VALUE)
    pairs = jnp.concatenate([toep[:, :-1], toep[:, 1:]], axis=-1)
    meta = jnp.pad(meta_bias.astype(F32) * LOG2_E, ((0, 0), (GRID_W, GRID_W - N_META)),
                   constant_values=MASK_VALUE).reshape(N_HEADS, 1, PAIR)
    heads_per_step = 4
    tab = pl.pallas_call(
        functools.partial(_bias_body, heads=heads_per_step),
        grid=(N_VARIANTS, N_HEADS // heads_per_step),
        in_specs=[
            pl.BlockSpec((heads_per_step, 2 * KH, GRID_W, PAIR), lambda v, g: (g, 0, 0, 0)),
            pl.BlockSpec((heads_per_step, 1, PAIR), lambda v, g: (g, 0, 0)),
        ],
        out_specs=pl.BlockSpec((1, heads_per_step, BLOCK_Q, ATTN_KEYS), lambda v, g: (v, g, 0, 0)),
        out_shape=jax.ShapeDtypeStruct((N_VARIANTS, N_HEADS, BLOCK_Q, ATTN_KEYS), F32),
        compiler_params=_params(2),
        name="bias_table",
    )(pairs, meta)
    return tab.reshape(N_VARIANTS, N_HEADS // 2, 2 * BLOCK_Q, ATTN_KEYS)


def _bias_body(pairs_ref, meta_ref, out_ref, *, heads):
    v = pl.program_id(0)
    offset = jnp.where(v == 0, 0, jnp.where(v == 1, KH // 2, KEY_ROWS - ROWS_PER_BLOCK))
    left = lax.broadcasted_iota(jnp.int32, (GRID_W, PAIR), 1) < GRID_W
    for h in range(heads):
        meta_tile = jnp.broadcast_to(meta_ref[h], (GRID_W, PAIR))
        for ri in range(ROWS_PER_BLOCK):
            lo = jnp.where(v == 0, 0, jnp.where(v == 1, ri, KH // 2 - 1))
            for j in range(ATTN_KEYS // PAIR):
                kr = 2 * j
                dr = kr - offset - ri + (KH - 1)
                tile = pairs_ref[h, jnp.clip(dr + 1, 0, 2 * KH - 1)]
                in_left = jnp.logical_and(kr >= lo, kr < lo + KH)
                in_right = jnp.logical_and(kr + 1 >= lo, kr + 1 < lo + KH)
                if kr + 1 < KEY_ROWS:
                    right_tile = jnp.where(in_right, tile, MASK_VALUE)
                else:
                    right_tile = meta_tile
                tile = jnp.where(left, jnp.where(in_left, tile, MASK_VALUE), right_tile)
                out_ref[0, h, ri * GRID_W:(ri + 1) * GRID_W, j * PAIR:(j + 1) * PAIR] = tile


def _pool_mixer(ext_ref, sum_ref, r0, n, p0, seq_len, wp_ref, ps_ref):
    p = p0 + lax.broadcasted_iota(jnp.int32, (n, 1), 0)
    outs = []
    for g, w in enumerate(POOL_WINDOWS):
        cs = slice(g * POOL_GROUP_DIM, (g + 1) * POOL_GROUP_DIM)
        span = n + 2 * HALO
        s = ext_ref[pl.ds(r0, span), cs]
        width = 1
        while width < w:
            s = s + pltpu.roll(s, span - width, axis=0)
            width *= 2
        sum_ref[:, cs] = s
        cnt = (jnp.minimum(p + w // 2, seq_len) - jnp.maximum(p - w // 2, 0)).astype(F32)
        m = sum_ref[pl.ds(HALO - w // 2, n), cs] / cnt - ext_ref[pl.ds(r0 + HALO, n), cs]
        y = jnp.dot(m.astype(BF16), wp_ref[g], preferred_element_type=F32) * ps_ref[:, cs]
        outs.append(y.astype(BF16))
    return jnp.concatenate(outs, axis=1)


def _mix_body(u_ref, up_ref, un_ref, um_ref, ya_ref, yam_ref, h_ref, hm_ref,
              wp_ref, ps_ref, wo_ref, g_ref, b_ref, h1_ref, h1m_ref,
              ext_ref, sum_ref, extm_ref, summ_ref, cat_ref, *, tile, seq_len):
    i = pl.program_id(1)
    last = pl.num_programs(1) - 1
    um = um_ref[0]
    prev = jnp.where(i == 0, um[N_META - HALO:], up_ref[0])
    nxt = jnp.where(i == last, jnp.zeros((HALO, D_POOL), F32), un_ref[0])
    ext_ref[...] = jnp.concatenate([prev, u_ref[0], nxt], axis=0)
    sub = tile // MIX_SUBTILES
    for r in range(MIX_SUBTILES):
        rows = slice(r * sub, (r + 1) * sub)
        cat_ref[rows, :D_POOL] = _pool_mixer(
            ext_ref, sum_ref.at[r], r * sub, sub, N_META + i * tile + r * sub, seq_len,
            wp_ref, ps_ref)
        cat_ref[rows, D_POOL:] = ya_ref[0, rows, :]
    for r in range(MIX_SUBTILES):
        rows = slice(r * sub, (r + 1) * sub)
        mix = jnp.dot(cat_ref[rows, :], wo_ref[...], preferred_element_type=F32)
        h1_ref[0, rows, :] = _layernorm(ALPHA * h_ref[0, rows, :] + mix, g_ref[...], b_ref[...])

    @pl.when(i == 0)
    def _():
        extm_ref[...] = jnp.concatenate(
            [jnp.zeros((HALO, D_POOL), F32), um, u_ref[0, :HALO, :]], axis=0)
        y_pool_m = _pool_mixer(extm_ref, summ_ref, 0, N_META, 0, seq_len, wp_ref, ps_ref)
        cat_m = jnp.concatenate([y_pool_m, yam_ref[0]], axis=1)
        mix_m = jnp.dot(cat_m, wo_ref[...], preferred_element_type=F32)
        h1m_ref[0] = _layernorm(ALPHA * hm_ref[0] + mix_m, g_ref[...], b_ref[...])


def _halo_specs(tile, n_tokens, width):
    per_tile = tile // HALO
    n_blocks = n_tokens // HALO
    prev = pl.BlockSpec((1, HALO, width),
                        lambda b, i, *_: (b, jnp.maximum(i * per_tile - 1, 0), 0))
    nxt = pl.BlockSpec((1, HALO, width),
                       lambda b, i, *_: (b, jnp.minimum((i + 1) * per_tile, n_blocks - 1), 0))
    return prev, nxt


def _mix(u, um, ya, yam, h, hm, w_pool_bf16, pool_scale, w_out_bf16, ln_g, ln_b, tile):
    B, T, D = h.shape
    prev_spec, next_spec = _halo_specs(tile, T, D_POOL)
    meta = lambda width: pl.BlockSpec((1, N_META, width), lambda b, i: (0, 0, 0))
    meta_b = lambda width: pl.BlockSpec((1, N_META, width), lambda b, i: (b, 0, 0))
    return pl.pallas_call(
        functools.partial(_mix_body, tile=tile, seq_len=N_META + T),
        grid=(B, T // tile),
        in_specs=[
            pl.BlockSpec((1, tile, D_POOL), lambda b, i: (b, i, 0)),
            prev_spec, next_spec,
            meta(D_POOL),
            pl.BlockSpec((1, tile, D_ATTN), lambda b, i: (b, i, 0)),
            meta_b(D_ATTN),
            pl.BlockSpec((1, tile, D), lambda b, i: (b, i, 0)),
            meta(D),
            _const_spec(w_pool_bf16.shape),
            _const_spec((1, D_POOL)),
            _const_spec((D, D)),
            _const_spec((1, D)),
            _const_spec((1, D)),
        ],
        out_specs=[
            pl.BlockSpec((1, tile, D), lambda b, i: (b, i, 0)),
            meta_b(D),
        ],
        out_shape=[
            jax.ShapeDtypeStruct((B, T, D), F32),
            jax.ShapeDtypeStruct((B, N_META, D), F32),
        ],
        scratch_shapes=[
            pltpu.VMEM((tile + 2 * HALO, D_POOL), F32),
            pltpu.VMEM((MIX_SUBTILES, tile // MIX_SUBTILES + 2 * HALO, D_POOL), F32),
            pltpu.VMEM((N_META + 2 * HALO, D_POOL), F32),
            pltpu.VMEM((N_META + 2 * HALO, D_POOL), F32),
            pltpu.VMEM((tile, D), BF16),
        ],
        compiler_params=_params(2),
        name="mix",
    )(u, u, u, um, ya, yam, h, hm, w_pool_bf16, pool_scale, w_out_bf16, ln_g, ln_b)


def _ffn_body(h1_ref, hp_ref, hn_ref, h1m_ref, wa_ref, wg_ref, par_ref, wd_ref,
              g_ref, b_ref, o_ref, lhs_ref, z_ref, acc_ref, *, tile):
    i = pl.program_id(1)
    c = pl.program_id(2)
    last_tile = pl.num_programs(1) - 1
    last_chunk = pl.num_programs(2) - 1
    n_sub = FF_CHUNK // FF_SUB

    @pl.when(c == 0)
    def _():
        prev = jnp.where(i == 0, h1m_ref[0, N_META - HALO:, :], hp_ref[0])
        lhs_ref[...] = jnp.concatenate([prev, h1_ref[0], hn_ref[0]], axis=0).astype(BF16)
        acc_ref[...] = jnp.zeros_like(acc_ref)

    is_seq_end = i == last_tile
    sqrt_half = np.sqrt(0.5).astype(np.float32)

    def conv(half, s, row0, n_rows):
        cs = slice(s * FF_SUB, (s + 1) * FF_SUB)
        z = z_ref.at[half * n_sub + s]
        taps = [par_ref[half, j:j + 1, cs] for j in range(3)]
        const = (taps[0] + taps[1] + taps[2]) * par_ref[half, 4:5, cs] + par_ref[half, 3:4, cs]
        return (z[pl.ds(HALO - 1 + row0, n_rows), :] * taps[0]
                + z[pl.ds(HALO + row0, n_rows), :] * taps[1]
                + z[pl.ds(HALO + 1 + row0, n_rows), :] * taps[2] + const)

    def gated(s, row0, n_rows):
        a = conv(0, s, row0, n_rows)
        g = conv(1, s, row0, n_rows)
        return (a * (0.5 * g * (1.0 + lax.erf(g * sqrt_half)))).astype(BF16)

    lhs = lhs_ref[...]
    for s in range(n_sub):
        cs = slice(s * FF_SUB, (s + 1) * FF_SUB)
        for half, w_ref in enumerate((wa_ref, wg_ref)):
            z = jnp.dot(lhs, w_ref[:, cs], preferred_element_type=F32)
            z_ref[half * n_sub + s] = z
            z_ref[half * n_sub + s, HALO + tile:, :] = jnp.where(
                is_seq_end, -par_ref[half, 4:5, cs], z[HALO + tile:, :])

    down = None
    for s in range(n_sub - 1):
        part = jnp.dot(gated(s, 0, tile), wd_ref[s * FF_SUB:(s + 1) * FF_SUB, :],
                       preferred_element_type=F32)
        down = part if down is None else down + part
    half_rows = tile // 2
    wd_last = wd_ref[(n_sub - 1) * FF_SUB:, :]
    for r in range(2):
        rows = slice(r * half_rows, (r + 1) * half_rows)
        part = jnp.dot(gated(n_sub - 1, r * half_rows, half_rows), wd_last,
                       preferred_element_type=F32)
        acc_ref[rows, :] += part if down is None else down[rows] + part

    @pl.when(c == last_chunk)
    def _():
        o_ref[0] = _layernorm(ALPHA * h1_ref[0] + acc_ref[...], g_ref[...], b_ref[...])


def _ffn(h1, h1m, w_up_p, ff_par, w_down_p, ln_g, ln_b, tile):
    B, T, D = h1.shape
    prev_spec, next_spec = _halo_specs(tile, T, D)
    const3 = lambda shape: pl.BlockSpec(shape, lambda b, i, c: (0,) * len(shape),
                                        pipeline_mode=pl.Buffered(1))
    n_par = ff_par.shape[1]
    return pl.pallas_call(
        functools.partial(_ffn_body, tile=tile),
        grid=(B, T // tile, N_FF_CHUNKS),
        in_specs=[
            pl.BlockSpec((1, tile, D), lambda b, i, c: (b, i, 0)),
            prev_spec, next_spec,
            pl.BlockSpec((1, N_META, D), lambda b, i, c: (b, 0, 0)),
            pl.BlockSpec((D, FF_CHUNK), lambda b, i, c: (0, c)),
            pl.BlockSpec((D, FF_CHUNK), lambda b, i, c: (0, N_FF_CHUNKS + c)),
            pl.BlockSpec((2, n_par, FF_CHUNK), lambda b, i, c: (0, 0, c)),
            pl.BlockSpec((FF_CHUNK, D), lambda b, i, c: (c, 0)),
            const3((1, D)), const3((1, D)),
        ],
        out_specs=pl.BlockSpec((1, tile, D), lambda b, i, c: (b, i, 0)),
        out_shape=jax.ShapeDtypeStruct((B, T, D), F32),
        scratch_shapes=[
            pltpu.VMEM((tile + 2 * HALO, D), BF16),
            pltpu.VMEM((2 * FF_CHUNK // FF_SUB, tile + 2 * HALO, FF_SUB), F32),
            pltpu.VMEM((tile, D), F32),
        ],
        compiler_params=_params(3),
        name="ffn",
    )(h1, h1, h1, h1m, w_up_p, w_up_p, ff_par, w_down_p, ln_g, ln_b)


def _pad_ff_cols(a):
    zeros = jnp.zeros(a.shape[:-1] + (D_FF_PAD - D_FF,), a.dtype)
    return jnp.concatenate([a[..., :D_FF], zeros, a[..., D_FF:], zeros], axis=-1)


def _w_up_body(w_ref, o_ref):
    zeros = jnp.zeros((w_ref.shape[0], D_FF_PAD - D_FF), BF16)
    o_ref[:, :D_FF] = w_ref[:, :D_FF].astype(BF16)
    o_ref[:, D_FF:D_FF_PAD] = zeros
    o_ref[:, D_FF_PAD:D_FF_PAD + D_FF] = w_ref[:, D_FF:].astype(BF16)
    o_ref[:, D_FF_PAD + D_FF:] = zeros


def _prep_w_up(w_up):
    D = w_up.shape[0]
    rows = 128
    return pl.pallas_call(
        _w_up_body,
        grid=(D // rows,),
        in_specs=[pl.BlockSpec((rows, 2 * D_FF), lambda i: (i, 0))],
        out_specs=pl.BlockSpec((rows, 2 * D_FF_PAD), lambda i: (i, 0)),
        out_shape=jax.ShapeDtypeStruct((D, 2 * D_FF_PAD), BF16),
        compiler_params=_params(1),
        name="prep_w_up",
    )(w_up)


def _split_halves(a):
    return jnp.transpose(a.reshape(a.shape[0], 2, D_FF_PAD), (1, 0, 2))


def _encode(x, meta, consts):
    (ln_in_g, ln_in_b, w_in, w_pool, pool_scale, bias_tab, w_out, ln1_g, ln1_b,
     w_up, ff_par, w_down, ln2_g, ln2_b) = consts
    hm, um, qm, km, vm = meta
    h, u, q, k, v = _inproj(x, ln_in_g, ln_in_b, w_in, TOKEN_TILE, INPROJ_SUBTILES)
    ya, yam = _attention(q, k, v, qm, km, vm, bias_tab)
    h1, h1m = _mix(u, um, ya, yam, h, hm, w_pool, pool_scale, w_out, ln1_g, ln1_b, TOKEN_TILE)
    return _ffn(h1, h1m, w_up, ff_par, w_down, ln2_g, ln2_b, TOKEN_TILE)


def kernel(x_prompt, x_sample, meta_tokens, ln_in_g, ln_in_b, w_in, w_pool, pool_scale, rpb,
           meta_bias, w_out, ln1_g, ln1_b, w_up, b_up, conv_w, conv_b, w_down, ln2_g, ln2_b):
    row = lambda a: a.reshape(1, -1).astype(F32)
    bias_tab = _bias_table(rpb[0], meta_bias[0])
    w_down_p = jnp.concatenate(
        [w_down[0].astype(BF16), jnp.zeros((D_FF_PAD - D_FF, D_MODEL), BF16)], axis=0)
    consts = (
        row(ln_in_g), row(ln_in_b), w_in[0].astype(BF16), w_pool[0].astype(BF16),
        row(pool_scale[0]), bias_tab, w_out[0].astype(BF16), row(ln1_g[0]), row(ln1_b[0]),
        _prep_w_up(w_up[0].astype(F32)),
        _split_halves(_pad_ff_cols(jnp.concatenate(
            [conv_w[0].astype(F32), row(conv_b[0]), row(b_up[0])], axis=0))),
        w_down_p, row(ln2_g[0]), row(ln2_b[0]),
    )
    meta = _inproj(meta_tokens[None].astype(F32), consts[0], consts[1], consts[2], N_META, 1)
    y_prompt = _encode(x_prompt, meta, consts)
    y_sample = _encode(x_sample, meta, consts)
    return (y_prompt, y_sample)
```

```python
import functools

import numpy as np
import jax
import jax.numpy as jnp
from jax import lax
from jax.experimental import pallas as pl
from jax.experimental.pallas import tpu as pltpu

F32 = jnp.float32
BF16 = jnp.bfloat16

D_MODEL = 2048
N_META = 16
GRID_W = 64
KH = 8
KW = 16
D_POOL = D_MODEL // 2
D_ATTN = D_MODEL - D_POOL
POOL_WINDOWS = (2, 4, 8, 16)
POOL_GROUP_DIM = D_POOL // len(POOL_WINDOWS)
HEAD_DIM = 64
N_HEADS = D_ATTN // HEAD_DIM
D_FF = 5504
LN_EPS = 1e-5
DEPTH = 1
ALPHA = float((2 * DEPTH) ** 0.25)
QK_SCALE = HEAD_DIM ** -0.5
LOG2_E = float(np.log2(np.e))
MASK_VALUE = -1e30

LANES = 128
SUBLANES = 8
PAIR = 2 * HEAD_DIM
HEAD_GROUP = 256
N_HEAD_GROUPS = D_ATTN // HEAD_GROUP
PAIRS_PER_GROUP = HEAD_GROUP // PAIR
HALO = SUBLANES
VMEM_LIMIT_BYTES = 56 * 1024 * 1024

TOKEN_TILE = 512
MIX_SUBTILES = 2
INPROJ_SUBTILES = 2
ROWS_PER_STEP = 32
ROWS_PER_BLOCK = 4
KEY_ROWS = ROWS_PER_BLOCK + KH - 1
ATTN_KEYS = (KEY_ROWS + 1) * GRID_W
BLOCK_Q = ROWS_PER_BLOCK * GRID_W
N_VARIANTS = 3
FF_CHUNK = 512
D_FF_PAD = -(-D_FF // FF_CHUNK) * FF_CHUNK
N_FF_CHUNKS = D_FF_PAD // FF_CHUNK
FF_SUB = 256


def _layernorm(x, g, b):
    mu = jnp.mean(x, axis=-1, keepdims=True)
    xc = x - mu
    var = jnp.mean(xc * xc, axis=-1, keepdims=True)
    return xc * lax.rsqrt(var + LN_EPS) * g + b


def _const_spec(shape):
    zeros = (0,) * len(shape)
    return pl.BlockSpec(shape, lambda *_: zeros, pipeline_mode=pl.Buffered(1))


def _params(n_axes, flags=None):
    return pltpu.CompilerParams(
        dimension_semantics=("arbitrary",) * n_axes,
        vmem_limit_bytes=VMEM_LIMIT_BYTES,
        flags=flags,
    )


def _inproj_body(x_ref, g_ref, b_ref, w_ref, h_ref, u_ref, q_ref, k_ref, v_ref, hb_ref,
                 *, subtiles):
    sub = x_ref.shape[1] // subtiles

    def norm(r):
        rows = slice(r * sub, (r + 1) * sub)
        h = _layernorm(x_ref[0, rows, :], g_ref[...], b_ref[...])
        h_ref[0, rows, :] = h
        hb_ref[rows, :] = h.astype(BF16)

    def pool_proj(r):
        rows = slice(r * sub, (r + 1) * sub)
        u_ref[0, rows, :] = jnp.dot(hb_ref[rows, :], w_ref[:, 0:D_POOL],
                                    preferred_element_type=F32)

    def qkv_proj(r):
        rows = slice(r * sub, (r + 1) * sub)
        hb = hb_ref[rows, :]
        for j in range(N_HEAD_GROUPS):
            lo = D_POOL + j * HEAD_GROUP
            q = jnp.dot(hb, w_ref[:, lo:lo + HEAD_GROUP], preferred_element_type=F32)
            q_ref[0, j, rows, :] = (q * (QK_SCALE * LOG2_E)).astype(BF16)
            lo += D_ATTN
            k_ref[0, j, rows, :] = jnp.dot(hb, w_ref[:, lo:lo + HEAD_GROUP],
                                           preferred_element_type=F32).astype(BF16)
            lo += D_ATTN
            v_ref[0, j, rows, :] = jnp.dot(hb, w_ref[:, lo:lo + HEAD_GROUP],
                                           preferred_element_type=F32).astype(BF16)

    norm(0)
    for r in range(subtiles):
        pool_proj(r)
        if r + 1 < subtiles:
            norm(r + 1)
        qkv_proj(r)


def _inproj(x, ln_g, ln_b, w_in_bf16, tile, subtiles):
    B, T, D = x.shape
    n_in = w_in_bf16.shape[1]
    hg_shape = (B, N_HEAD_GROUPS, T, HEAD_GROUP)
    hg_spec = pl.BlockSpec((1, N_HEAD_GROUPS, tile, HEAD_GROUP), lambda b, i: (b, 0, i, 0))
    return pl.pallas_call(
        functools.partial(_inproj_body, subtiles=subtiles),
        grid=(B, T // tile),
        in_specs=[
            pl.BlockSpec((1, tile, D), lambda b, i: (b, i, 0)),
            _const_spec((1, D)),
            _const_spec((1, D)),
            _const_spec((D, n_in)),
        ],
        out_specs=[
            pl.BlockSpec((1, tile, D), lambda b, i: (b, i, 0)),
            pl.BlockSpec((1, tile, D_POOL), lambda b, i: (b, i, 0)),
            hg_spec, hg_spec, hg_spec,
        ],
        out_shape=[
            jax.ShapeDtypeStruct((B, T, D), F32),
            jax.ShapeDtypeStruct((B, T, D_POOL), F32),
            jax.ShapeDtypeStruct(hg_shape, BF16),
            jax.ShapeDtypeStruct(hg_shape, BF16),
            jax.ShapeDtypeStruct(hg_shape, BF16),
        ],
        scratch_shapes=[pltpu.VMEM((tile, D), BF16)],
        compiler_params=_params(2),
        name="inproj",
    )(x, ln_g, ln_b, w_in_bf16)


def _attend(q2, keys, vals, bias):
    m = q2.shape[0]
    first = lax.broadcasted_iota(jnp.int32, (m, PAIR), 1) < HEAD_DIM
    zero = jnp.zeros_like(q2)
    qs = jnp.concatenate([jnp.where(first, q2, zero), jnp.where(first, zero, q2)], axis=0)
    s = lax.dot_general(qs, keys, (((1,), (1,)), ((), ())), preferred_element_type=F32) + bias
    e = jnp.exp2(s - jnp.max(s, axis=-1, keepdims=True)).astype(BF16)
    vals_ones = jnp.concatenate([vals, jnp.ones_like(vals)], axis=1)
    o = jnp.dot(e, vals_ones, preferred_element_type=F32)
    o = o[:, :PAIR] / o[:, PAIR:]
    return jnp.where(first, o[:m], o[m:])


def _attn_body(q_ref, k_ref, v_ref, qm_ref, km_ref, vm_ref, bias_ref, y_ref, ym_ref, *, rows):
    rb = pl.program_id(2)
    n_real = KEY_ROWS * GRID_W
    pad = jnp.zeros((GRID_W - N_META, PAIR), BF16)

    def key_set(ref, meta_ref, k_off, ls):
        return jnp.concatenate(
            [ref[0, 0, pl.ds(k_off, n_real), ls], meta_ref[0, 0, :, ls], pad], axis=0)

    for blk in range(ROWS_PER_STEP // ROWS_PER_BLOCK):
        r0 = rb * ROWS_PER_STEP + blk * ROWS_PER_BLOCK
        ks = jnp.clip(r0 - KH // 2, 0, rows - KEY_ROWS)
        variant = jnp.where(r0 == 0, 0, jnp.where(r0 == rows - ROWS_PER_BLOCK, 2, 1))
        k_off = pl.multiple_of(ks * GRID_W, GRID_W)
        qs = slice(blk * BLOCK_Q, (blk + 1) * BLOCK_Q)
        for p in range(PAIRS_PER_GROUP):
            ls = slice(p * PAIR, (p + 1) * PAIR)
            out = _attend(q_ref[0, 0, qs, ls], key_set(k_ref, km_ref, k_off, ls),
                          key_set(v_ref, vm_ref, k_off, ls), bias_ref[variant, p])
            y_ref[0, qs, ls] = out.astype(BF16)

    @pl.when(rb == 0)
    def _():
        for p in range(PAIRS_PER_GROUP):
            ls = slice(p * PAIR, (p + 1) * PAIR)
            bias = jnp.concatenate(
                [jnp.broadcast_to(bias_ref[0, p, 0:1, :], (N_META, ATTN_KEYS)),
                 jnp.broadcast_to(bias_ref[0, p, BLOCK_Q:BLOCK_Q + 1, :], (N_META, ATTN_KEYS))],
                axis=0)
            out = _attend(qm_ref[0, 0, :, ls], key_set(k_ref, km_ref, 0, ls),
                          key_set(v_ref, vm_ref, 0, ls), bias)
            ym_ref[0, :, ls] = out.astype(BF16)


def _attention(q, k, v, qm, km, vm, bias_tab):
    B, _, T, _ = q.shape
    rows = T // GRID_W
    assert rows % ROWS_PER_STEP == 0 and rows >= KEY_ROWS + 1
    q_rows = ROWS_PER_STEP * GRID_W
    meta_spec = pl.BlockSpec((1, 1, N_META, HEAD_GROUP), lambda b, g, i: (0, g, 0, 0))
    kv_spec = pl.BlockSpec((1, 1, T, HEAD_GROUP), lambda b, g, i: (b, g, 0, 0))
    return pl.pallas_call(
        functools.partial(_attn_body, rows=rows),
        grid=(B, N_HEAD_GROUPS, rows // ROWS_PER_STEP),
        in_specs=[
            pl.BlockSpec((1, 1, q_rows, HEAD_GROUP), lambda b, g, i: (b, g, i, 0)),
            kv_spec, kv_spec,
            meta_spec, meta_spec, meta_spec,
            pl.BlockSpec((N_VARIANTS, PAIRS_PER_GROUP, 2 * BLOCK_Q, ATTN_KEYS),
                         lambda b, g, i: (0, g, 0, 0)),
        ],
        out_specs=[
            pl.BlockSpec((1, q_rows, HEAD_GROUP), lambda b, g, i: (b, i, g)),
            pl.BlockSpec((1, N_META, HEAD_GROUP), lambda b, g, i: (b, 0, g)),
        ],
        out_shape=[
            jax.ShapeDtypeStruct((B, T, D_ATTN), BF16),
            jax.ShapeDtypeStruct((B, N_META, D_ATTN), BF16),
        ],
        compiler_params=_params(3),
        name="attention",
    )(q, k, v, qm, km, vm, bias_tab)


def _bias_table(rpb, meta_bias):
    c = np.arange(GRID_W)
    kc = np.arange(GRID_W)
    cs = np.clip(c - KW // 2, 0, GRID_W - KW)
    valid = (kc[None, :] >= cs[:, None]) & (kc[None, :] < cs[:, None] + KW)
    dc = kc[None, :] - c[:, None] + (KW - 1)
    onehot = ((dc[None] == np.arange(2 * KW - 1)[:, None, None]) & valid[None]).astype(np.float32)
    toep = jnp.einsum('hrj,jck->hrck', rpb.astype(F32) * LOG2_E, onehot,
                      precision=lax.Precision.HIGHEST)
    toep = jnp.where(valid[None, None], toep, MASK_VALUE)
    toep = jnp.pad(toep, ((0, 0), (1, 1), (0, 0), (0, 0)), constant_values=MASK_VALUE)
    pairs = jnp.concatenate([toep[:, :-1], toep[:, 1:]], axis=-1)
    meta = jnp.pad(meta_bias.astype(F32) * LOG2_E, ((0, 0), (GRID_W, GRID_W - N_META)),
                   constant_values=MASK_VALUE).reshape(N_HEADS, 1, PAIR)
    heads_per_step = 4
    tab = pl.pallas_call(
        functools.partial(_bias_body, heads=heads_per_step),
        grid=(N_VARIANTS, N_HEADS // heads_per_step),
        in_specs=[
            pl.BlockSpec((heads_per_step, 2 * KH, GRID_W, PAIR), lambda v, g: (g, 0, 0, 0)),
            pl.BlockSpec((heads_per_step, 1, PAIR), lambda v, g: (g, 0, 0)),
        ],
        out_specs=pl.BlockSpec((1, heads_per_step, BLOCK_Q, ATTN_KEYS), lambda v, g: (v, g, 0, 0)),
        out_shape=jax.ShapeDtypeStruct((N_VARIANTS, N_HEADS, BLOCK_Q, ATTN_KEYS), F32),
        compiler_params=_params(2),
        name="bias_table",
    )(pairs, meta)
    return tab.reshape(N_VARIANTS, N_HEADS // 2, 2 * BLOCK_Q, ATTN_KEYS)


def _bias_body(pairs_ref, meta_ref, out_ref, *, heads):
    v = pl.program_id(0)
    offset = jnp.where(v == 0, 0, jnp.where(v == 1, KH // 2, KEY_ROWS - ROWS_PER_BLOCK))
    left = lax.broadcasted_iota(jnp.int32, (GRID_W, PAIR), 1) < GRID_W
    for h in range(heads):
        meta_tile = jnp.broadcast_to(meta_ref[h], (GRID_W, PAIR))
        for ri in range(ROWS_PER_BLOCK):
            lo = jnp.where(v == 0, 0, jnp.where(v == 1, ri, KH // 2 - 1))
            for j in range(ATTN_KEYS // PAIR):
                kr = 2 * j
                dr = kr - offset - ri + (KH - 1)
                tile = pairs_ref[h, jnp.clip(dr + 1, 0, 2 * KH - 1)]
                in_left = jnp.logical_and(kr >= lo, kr < lo + KH)
                in_right = jnp.logical_and(kr + 1 >= lo, kr + 1 < lo + KH)
                if kr + 1 < KEY_ROWS:
                    right_tile = jnp.where(in_right, tile, MASK_VALUE)
                else:
                    right_tile = meta_tile
                tile = jnp.where(left, jnp.where(in_left, tile, MASK_VALUE), right_tile)
                out_ref[0, h, ri * GRID_W:(ri + 1) * GRID_W, j * PAIR:(j + 1) * PAIR] = tile


def _pool_mixer(ext_ref, sum_ref, r0, n, p0, seq_len, wp_ref, ps_ref):
    p = p0 + lax.broadcasted_iota(jnp.int32, (n, 1), 0)
    outs = []
    for g, w in enumerate(POOL_WINDOWS):
        cs = slice(g * POOL_GROUP_DIM, (g + 1) * POOL_GROUP_DIM)
        span = n + 2 * HALO
        s = ext_ref[pl.ds(r0, span), cs]
        width = 1
        while width < w:
            s = s + pltpu.roll(s, span - width, axis=0)
            width *= 2
        sum_ref[:, cs] = s
        cnt = (jnp.minimum(p + w // 2, seq_len) - jnp.maximum(p - w // 2, 0)).astype(F32)
        m = sum_ref[pl.ds(HALO - w // 2, n), cs] / cnt - ext_ref[pl.ds(r0 + HALO, n), cs]
        y = jnp.dot(m.astype(BF16), wp_ref[g], preferred_element_type=F32) * ps_ref[:, cs]
        outs.append(y.astype(BF16))
    return jnp.concatenate(outs, axis=1)


def _mix_body(u_ref, up_ref, un_ref, um_ref, ya_ref, yam_ref, h_ref, hm_ref,
              wp_ref, ps_ref, wo_ref, g_ref, b_ref, h1_ref, h1m_ref,
              ext_ref, sum_ref, extm_ref, summ_ref, cat_ref, *, tile, seq_len):
    i = pl.program_id(1)
    last = pl.num_programs(1) - 1
    um = um_ref[0]
    prev = jnp.where(i == 0, um[N_META - HALO:], up_ref[0])
    nxt = jnp.where(i == last, jnp.zeros((HALO, D_POOL), F32), un_ref[0])
    ext_ref[...] = jnp.concatenate([prev, u_ref[0], nxt], axis=0)
    sub = tile // MIX_SUBTILES
    for r in range(MIX_SUBTILES):
        rows = slice(r * sub, (r + 1) * sub)
        cat_ref[rows, :D_POOL] = _pool_mixer(
            ext_ref, sum_ref.at[r], r * sub, sub, N_META + i * tile + r * sub, seq_len,
            wp_ref, ps_ref)
        cat_ref[rows, D_POOL:] = ya_ref[0, rows, :]
    for r in range(MIX_SUBTILES):
        rows = slice(r * sub, (r + 1) * sub)
        mix = jnp.dot(cat_ref[rows, :], wo_ref[...], preferred_element_type=F32)
        h1_ref[0, rows, :] = _layernorm(ALPHA * h_ref[0, rows, :] + mix, g_ref[...], b_ref[...])

    @pl.when(i == 0)
    def _():
        extm_ref[...] = jnp.concatenate(
            [jnp.zeros((HALO, D_POOL), F32), um, u_ref[0, :HALO, :]], axis=0)
        y_pool_m = _pool_mixer(extm_ref, summ_ref, 0, N_META, 0, seq_len, wp_ref, ps_ref)
        cat_m = jnp.concatenate([y_pool_m, yam_ref[0]], axis=1)
        mix_m = jnp.dot(cat_m, wo_ref[...], preferred_element_type=F32)
        h1m_ref[0] = _layernorm(ALPHA * hm_ref[0] + mix_m, g_ref[...], b_ref[...])


def _halo_specs(tile, n_tokens, width):
    per_tile = tile // HALO
    n_blocks = n_tokens // HALO
    prev = pl.BlockSpec((1, HALO, width),
                        lambda b, i, *_: (b, jnp.maximum(i * per_tile - 1, 0), 0))
    nxt = pl.BlockSpec((1, HALO, width),
                       lambda b, i, *_: (b, jnp.minimum((i + 1) * per_tile, n_blocks - 1), 0))
    return prev, nxt


def _mix(u, um, ya, yam, h, hm, w_pool_bf16, pool_scale, w_out_bf16, ln_g, ln_b, tile):
    B, T, D = h.shape
    prev_spec, next_spec = _halo_specs(tile, T, D_POOL)
    meta = lambda width: pl.BlockSpec((1, N_META, width), lambda b, i: (0, 0, 0))
    meta_b = lambda width: pl.BlockSpec((1, N_META, width), lambda b, i: (b, 0, 0))
    return pl.pallas_call(
        functools.partial(_mix_body, tile=tile, seq_len=N_META + T),
        grid=(B, T // tile),
        in_specs=[
            pl.BlockSpec((1, tile, D_POOL), lambda b, i: (b, i, 0)),
            prev_spec, next_spec,
            meta(D_POOL),
            pl.BlockSpec((1, tile, D_ATTN), lambda b, i: (b, i, 0)),
            meta_b(D_ATTN),
            pl.BlockSpec((1, tile, D), lambda b, i: (b, i, 0)),
            meta(D),
            _const_spec(w_pool_bf16.shape),
            _const_spec((1, D_POOL)),
            _const_spec((D, D)),
            _const_spec((1, D)),
            _const_spec((1, D)),
        ],
        out_specs=[
            pl.BlockSpec((1, tile, D), lambda b, i: (b, i, 0)),
            meta_b(D),
        ],
        out_shape=[
            jax.ShapeDtypeStruct((B, T, D), F32),
            jax.ShapeDtypeStruct((B, N_META, D), F32),
        ],
        scratch_shapes=[
            pltpu.VMEM((tile + 2 * HALO, D_POOL), F32),
            pltpu.VMEM((MIX_SUBTILES, tile // MIX_SUBTILES + 2 * HALO, D_POOL), F32),
            pltpu.VMEM((N_META + 2 * HALO, D_POOL), F32),
            pltpu.VMEM((N_META + 2 * HALO, D_POOL), F32),
            pltpu.VMEM((tile, D), BF16),
        ],
        compiler_params=_params(2),
        name="mix",
    )(u, u, u, um, ya, yam, h, hm, w_pool_bf16, pool_scale, w_out_bf16, ln_g, ln_b)


def _ffn_body(h1_ref, hp_ref, hn_ref, h1m_ref, wa_ref, wg_ref, par_ref, wd_ref,
              g_ref, b_ref, o_ref, lhs_ref, z_ref, acc_ref, *, tile):
    i = pl.program_id(1)
    c = pl.program_id(2)
    last_tile = pl.num_programs(1) - 1
    last_chunk = pl.num_programs(2) - 1
    n_sub = FF_CHUNK // FF_SUB

    @pl.when(c == 0)
    def _():
        prev = jnp.where(i == 0, h1m_ref[0, N_META - HALO:, :], hp_ref[0])
        lhs_ref[...] = jnp.concatenate([prev, h1_ref[0], hn_ref[0]], axis=0).astype(BF16)
        acc_ref[...] = jnp.zeros_like(acc_ref)

    is_seq_end = i == last_tile
    sqrt_half = np.sqrt(0.5).astype(np.float32)

    def conv(half, s, row0, n_rows):
        cs = slice(s * FF_SUB, (s + 1) * FF_SUB)
        z = z_ref.at[half * n_sub + s]
        taps = [par_ref[half, j:j + 1, cs] for j in range(3)]
        const = (taps[0] + taps[1] + taps[2]) * par_ref[half, 4:5, cs] + par_ref[half, 3:4, cs]
        return (z[pl.ds(HALO - 1 + row0, n_rows), :] * taps[0]
                + z[pl.ds(HALO + row0, n_rows), :] * taps[1]
                + z[pl.ds(HALO + 1 + row0, n_rows), :] * taps[2] + const)

    def gated(s, row0, n_rows):
        a = conv(0, s, row0, n_rows)
        g = conv(1, s, row0, n_rows)
        return (a * (0.5 * g * (1.0 + lax.erf(g * sqrt_half)))).astype(BF16)

    lhs = lhs_ref[...]
    for s in range(n_sub):
        cs = slice(s * FF_SUB, (s + 1) * FF_SUB)
        for half, w_ref in enumerate((wa_ref, wg_ref)):
            z = jnp.dot(lhs, w_ref[:, cs], preferred_element_type=F32)
            z_ref[half * n_sub + s] = z
            z_ref[half * n_sub + s, HALO + tile:, :] = jnp.where(
                is_seq_end, -par_ref[half, 4:5, cs], z[HALO + tile:, :])

    for s in range(n_sub - 1):
        acc_ref[...] += jnp.dot(gated(s, 0, tile), wd_ref[s * FF_SUB:(s + 1) * FF_SUB, :],
                                preferred_element_type=F32)
    half_rows = tile // 2
    wd_last = wd_ref[(n_sub - 1) * FF_SUB:, :]
    for r in range(2):
        rows = slice(r * half_rows, (r + 1) * half_rows)
        acc_ref[rows, :] += jnp.dot(gated(n_sub - 1, r * half_rows, half_rows), wd_last,
                                    preferred_element_type=F32)

    @pl.when(c == last_chunk)
    def _():
        o_ref[0] = _layernorm(ALPHA * h1_ref[0] + acc_ref[...], g_ref[...], b_ref[...])


def _ffn(h1, h1m, w_up_p, ff_par, w_down_p, ln_g, ln_b, tile):
    B, T, D = h1.shape
    prev_spec, next_spec = _halo_specs(tile, T, D)
    const3 = lambda shape: pl.BlockSpec(shape, lambda b, i, c: (0,) * len(shape),
                                        pipeline_mode=pl.Buffered(1))
    n_par = ff_par.shape[1]
    return pl.pallas_call(
        functools.partial(_ffn_body, tile=tile),
        grid=(B, T // tile, N_FF_CHUNKS),
        in_specs=[
            pl.BlockSpec((1, tile, D), lambda b, i, c: (b, i, 0)),
            prev_spec, next_spec,
            pl.BlockSpec((1, N_META, D), lambda b, i, c: (b, 0, 0)),
            pl.BlockSpec((D, FF_CHUNK), lambda b, i, c: (0, c)),
            pl.BlockSpec((D, FF_CHUNK), lambda b, i, c: (0, N_FF_CHUNKS + c)),
            pl.BlockSpec((2, n_par, FF_CHUNK), lambda b, i, c: (0, 0, c)),
            pl.BlockSpec((FF_CHUNK, D), lambda b, i, c: (c, 0)),
            const3((1, D)), const3((1, D)),
        ],
        out_specs=pl.BlockSpec((1, tile, D), lambda b, i, c: (b, i, 0)),
        out_shape=jax.ShapeDtypeStruct((B, T, D), F32),
        scratch_shapes=[
            pltpu.VMEM((tile + 2 * HALO, D), BF16),
            pltpu.VMEM((2 * FF_CHUNK // FF_SUB, tile + 2 * HALO, FF_SUB), F32),
            pltpu.VMEM((tile, D), F32),
        ],
        compiler_params=_params(3),
        name="ffn",
    )(h1, h1, h1, h1m, w_up_p, w_up_p, ff_par, w_down_p, ln_g, ln_b)


def _pad_ff_cols(a):
    zeros = jnp.zeros(a.shape[:-1] + (D_FF_PAD - D_FF,), a.dtype)
    return jnp.concatenate([a[..., :D_FF], zeros, a[..., D_FF:], zeros], axis=-1)


def _w_up_body(w_ref, o_ref):
    zeros = jnp.zeros((w_ref.shape[0], D_FF_PAD - D_FF), BF16)
    o_ref[:, :D_FF] = w_ref[:, :D_FF].astype(BF16)
    o_ref[:, D_FF:D_FF_PAD] = zeros
    o_ref[:, D_FF_PAD:D_FF_PAD + D_FF] = w_ref[:, D_FF:].astype(BF16)
    o_ref[:, D_FF_PAD + D_FF:] = zeros


def _prep_w_up(w_up):
    D = w_up.shape[0]
    rows = 128
    return pl.pallas_call(
        _w_up_body,
        grid=(D // rows,),
        in_specs=[pl.BlockSpec((rows, 2 * D_FF), lambda i: (i, 0))],
        out_specs=pl.BlockSpec((rows, 2 * D_FF_PAD), lambda i: (i, 0)),
        out_shape=jax.ShapeDtypeStruct((D, 2 * D_FF_PAD), BF16),
        compiler_params=_params(1),
        name="prep_w_up",
    )(w_up)


def _w_down_body(w_ref, o_ref, *, n_src_blocks):
    keep = pl.program_id(0) < n_src_blocks
    o_ref[...] = jnp.where(keep, w_ref[...], 0.0).astype(BF16)


def _prep_w_down(w_down):
    D = w_down.shape[1]
    rows = D_FF_PAD - D_FF
    assert D_FF % rows == 0
    n_src = D_FF // rows
    return pl.pallas_call(
        functools.partial(_w_down_body, n_src_blocks=n_src),
        grid=(D_FF_PAD // rows,),
        in_specs=[pl.BlockSpec((rows, D), lambda i: (jnp.minimum(i, n_src - 1), 0))],
        out_specs=pl.BlockSpec((rows, D), lambda i: (i, 0)),
        out_shape=jax.ShapeDtypeStruct((D_FF_PAD, D), BF16),
        compiler_params=_params(1),
        name="prep_w_down",
    )(w_down)


def _split_halves(a):
    return jnp.transpose(a.reshape(a.shape[0], 2, D_FF_PAD), (1, 0, 2))


def _encode(x, meta, consts):
    (ln_in_g, ln_in_b, w_in, w_pool, pool_scale, bias_tab, w_out, ln1_g, ln1_b,
     w_up, ff_par, w_down, ln2_g, ln2_b) = consts
    hm, um, qm, km, vm = meta
    h, u, q, k, v = _inproj(x, ln_in_g, ln_in_b, w_in, TOKEN_TILE, INPROJ_SUBTILES)
    ya, yam = _attention(q, k, v, qm, km, vm, bias_tab)
    h1, h1m = _mix(u, um, ya, yam, h, hm, w_pool, pool_scale, w_out, ln1_g, ln1_b, TOKEN_TILE)
    return _ffn(h1, h1m, w_up, ff_par, w_down, ln2_g, ln2_b, TOKEN_TILE)


def kernel(x_prompt, x_sample, meta_tokens, ln_in_g, ln_in_b, w_in, w_pool, pool_scale, rpb,
           meta_bias, w_out, ln1_g, ln1_b, w_up, b_up, conv_w, conv_b, w_down, ln2_g, ln2_b):
    row = lambda a: a.reshape(1, -1).astype(F32)
    bias_tab = _bias_table(rpb[0], meta_bias[0])
    w_down_p = _prep_w_down(w_down[0].astype(F32))
    consts = (
        row(ln_in_g), row(ln_in_b), w_in[0].astype(BF16), w_pool[0].astype(BF16),
        row(pool_scale[0]), bias_tab, w_out[0].astype(BF16), row(ln1_g[0]), row(ln1_b[0]),
        _prep_w_up(w_up[0].astype(F32)),
        _split_halves(_pad_ff_cols(jnp.concatenate(
            [conv_w[0].astype(F32), row(conv_b[0]), row(b_up[0])], axis=0))),
        w_down_p, row(ln2_g[0]), row(ln2_b[0]),
    )
    meta = _inproj(meta_tokens[None].astype(F32), consts[0], consts[1], consts[2], N_META, 1)
    y_prompt = _encode(x_prompt, meta, consts)
    y_sample = _encode(x_sample, meta, consts)
    return (y_prompt, y_sample)
```

```python
import functools

import numpy as np
import jax
import jax.numpy as jnp
from jax import lax
from jax.experimental import pallas as pl
from jax.experimental.pallas import tpu as pltpu

F32 = jnp.float32
BF16 = jnp.bfloat16

D_MODEL = 2048
N_META = 16
GRID_W = 64
KH = 8
KW = 16
D_POOL = D_MODEL // 2
D_ATTN = D_MODEL - D_POOL
POOL_WINDOWS = (2, 4, 8, 16)
POOL_GROUP_DIM = D_POOL // len(POOL_WINDOWS)
HEAD_DIM = 64
N_HEADS = D_ATTN // HEAD_DIM
D_FF = 5504
LN_EPS = 1e-5
DEPTH = 1
ALPHA = float((2 * DEPTH) ** 0.25)
QK_SCALE = HEAD_DIM ** -0.5
LOG2_E = float(np.log2(np.e))
MASK_VALUE = -1e30

SUBLANES = 8
PAIR = 2 * HEAD_DIM
HEAD_GROUP = 256
N_HEAD_GROUPS = D_ATTN // HEAD_GROUP
PAIRS_PER_GROUP = HEAD_GROUP // PAIR
HALO = SUBLANES
VMEM_LIMIT_BYTES = 56 * 1024 * 1024

TOKEN_TILE = 512
MIX_SUBTILES = 2
INPROJ_SUBTILES = 2
ROWS_PER_STEP = 32
ROWS_PER_BLOCK = 4
KEY_ROWS = ROWS_PER_BLOCK + KH - 1
ATTN_KEYS = (KEY_ROWS + 1) * GRID_W
BLOCK_Q = ROWS_PER_BLOCK * GRID_W
N_VARIANTS = 3
BIAS_HEADS_PER_STEP = 8
PREP_ROWS = 256
FF_CHUNK = 512
D_FF_PAD = -(-D_FF // FF_CHUNK) * FF_CHUNK
N_FF_CHUNKS = D_FF_PAD // FF_CHUNK
FF_SUB = 256


def _layernorm(x, g, b):
    mu = jnp.mean(x, axis=-1, keepdims=True)
    xc = x - mu
    var = jnp.mean(xc * xc, axis=-1, keepdims=True)
    return xc * lax.rsqrt(var + LN_EPS) * g + b


def _const_spec(shape):
    zeros = (0,) * len(shape)
    return pl.BlockSpec(shape, lambda *_: zeros, pipeline_mode=pl.Buffered(1))


def _params(n_axes):
    return pltpu.CompilerParams(
        dimension_semantics=("arbitrary",) * n_axes,
        vmem_limit_bytes=VMEM_LIMIT_BYTES,
    )


def _inproj_body(x_ref, g_ref, b_ref, w_ref, h_ref, u_ref, q_ref, k_ref, v_ref, hb_ref,
                 *, subtiles):
    sub = x_ref.shape[1] // subtiles

    def norm(r):
        rows = slice(r * sub, (r + 1) * sub)
        h = _layernorm(x_ref[0, rows, :], g_ref[...], b_ref[...])
        h_ref[0, rows, :] = h
        hb_ref[rows, :] = h.astype(BF16)

    def pool_proj(r):
        rows = slice(r * sub, (r + 1) * sub)
        u_ref[0, rows, :] = jnp.dot(hb_ref[rows, :], w_ref[:, 0:D_POOL],
                                    preferred_element_type=F32)

    def qkv_proj(r):
        rows = slice(r * sub, (r + 1) * sub)
        hb = hb_ref[rows, :]
        for j in range(N_HEAD_GROUPS):
            lo = D_POOL + j * HEAD_GROUP
            q = jnp.dot(hb, w_ref[:, lo:lo + HEAD_GROUP], preferred_element_type=F32)
            q_ref[0, j, rows, :] = (q * (QK_SCALE * LOG2_E)).astype(BF16)
            lo += D_ATTN
            k_ref[0, j, rows, :] = jnp.dot(hb, w_ref[:, lo:lo + HEAD_GROUP],
                                           preferred_element_type=F32).astype(BF16)
            lo += D_ATTN
            v_ref[0, j, rows, :] = jnp.dot(hb, w_ref[:, lo:lo + HEAD_GROUP],
                                           preferred_element_type=F32).astype(BF16)

    norm(0)
    for r in range(subtiles):
        pool_proj(r)
        if r + 1 < subtiles:
            norm(r + 1)
        qkv_proj(r)


def _inproj(x, ln_g, ln_b, w_in_bf16, tile, subtiles):
    B, T, D = x.shape
    n_in = w_in_bf16.shape[1]
    hg_shape = (B, N_HEAD_GROUPS, T, HEAD_GROUP)
    hg_spec = pl.BlockSpec((1, N_HEAD_GROUPS, tile, HEAD_GROUP), lambda b, i: (b, 0, i, 0))
    return pl.pallas_call(
        functools.partial(_inproj_body, subtiles=subtiles),
        grid=(B, T // tile),
        in_specs=[
            pl.BlockSpec((1, tile, D), lambda b, i: (b, i, 0)),
            _const_spec((1, D)),
            _const_spec((1, D)),
            _const_spec((D, n_in)),
        ],
        out_specs=[
            pl.BlockSpec((1, tile, D), lambda b, i: (b, i, 0)),
            pl.BlockSpec((1, tile, D_POOL), lambda b, i: (b, i, 0)),
            hg_spec, hg_spec, hg_spec,
        ],
        out_shape=[
            jax.ShapeDtypeStruct((B, T, D), F32),
            jax.ShapeDtypeStruct((B, T, D_POOL), F32),
            jax.ShapeDtypeStruct(hg_shape, BF16),
            jax.ShapeDtypeStruct(hg_shape, BF16),
            jax.ShapeDtypeStruct(hg_shape, BF16),
        ],
        scratch_shapes=[pltpu.VMEM((tile, D), BF16)],
        compiler_params=_params(2),
        name="inproj",
    )(x, ln_g, ln_b, w_in_bf16)


def _attend(q2, keys, vals, bias):
    m = q2.shape[0]
    first = lax.broadcasted_iota(jnp.int32, (m, PAIR), 1) < HEAD_DIM
    zero = jnp.zeros_like(q2)
    qs = jnp.concatenate([jnp.where(first, q2, zero), jnp.where(first, zero, q2)], axis=0)
    s = lax.dot_general(qs, keys, (((1,), (1,)), ((), ())), preferred_element_type=F32) + bias
    e = jnp.exp2(s - jnp.max(s, axis=-1, keepdims=True)).astype(BF16)
    vals_ones = jnp.concatenate([vals, jnp.ones_like(vals)], axis=1)
    o = jnp.dot(e, vals_ones, preferred_element_type=F32)
    o = o[:, :PAIR] / o[:, PAIR:]
    return jnp.where(first, o[:m], o[m:])


def _attn_body(q_ref, k_ref, v_ref, qm_ref, km_ref, vm_ref, bias_ref, y_ref, ym_ref, *, rows):
    rb = pl.program_id(2)
    n_real = KEY_ROWS * GRID_W
    pad = jnp.zeros((GRID_W - N_META, PAIR), BF16)

    def key_set(ref, meta_ref, k_off, ls):
        return jnp.concatenate(
            [ref[0, 0, pl.ds(k_off, n_real), ls], meta_ref[0, 0, :, ls], pad], axis=0)

    for blk in range(ROWS_PER_STEP // ROWS_PER_BLOCK):
        r0 = rb * ROWS_PER_STEP + blk * ROWS_PER_BLOCK
        ks = jnp.clip(r0 - KH // 2, 0, rows - KEY_ROWS)
        variant = jnp.where(r0 == 0, 0, jnp.where(r0 == rows - ROWS_PER_BLOCK, 2, 1))
        k_off = pl.multiple_of(ks * GRID_W, GRID_W)
        qs = slice(blk * BLOCK_Q, (blk + 1) * BLOCK_Q)
        for p in range(PAIRS_PER_GROUP):
            ls = slice(p * PAIR, (p + 1) * PAIR)
            out = _attend(q_ref[0, 0, qs, ls], key_set(k_ref, km_ref, k_off, ls),
                          key_set(v_ref, vm_ref, k_off, ls), bias_ref[variant, p])
            y_ref[0, qs, ls] = out.astype(BF16)

    @pl.when(rb == 0)
    def _():
        for p in range(PAIRS_PER_GROUP):
            ls = slice(p * PAIR, (p + 1) * PAIR)
            bias = jnp.concatenate(
                [jnp.broadcast_to(bias_ref[0, p, 0:1, :], (N_META, ATTN_KEYS)),
                 jnp.broadcast_to(bias_ref[0, p, BLOCK_Q:BLOCK_Q + 1, :], (N_META, ATTN_KEYS))],
                axis=0)
            out = _attend(qm_ref[0, 0, :, ls], key_set(k_ref, km_ref, 0, ls),
                          key_set(v_ref, vm_ref, 0, ls), bias)
            ym_ref[0, :, ls] = out.astype(BF16)


def _attention(q, k, v, qm, km, vm, bias_tab):
    B, _, T, _ = q.shape
    rows = T // GRID_W
    assert rows % ROWS_PER_STEP == 0 and rows >= KEY_ROWS + 1
    q_rows = ROWS_PER_STEP * GRID_W
    meta_spec = pl.BlockSpec((1, 1, N_META, HEAD_GROUP), lambda b, g, i: (0, g, 0, 0))
    kv_spec = pl.BlockSpec((1, 1, T, HEAD_GROUP), lambda b, g, i: (b, g, 0, 0))
    return pl.pallas_call(
        functools.partial(_attn_body, rows=rows),
        grid=(B, N_HEAD_GROUPS, rows // ROWS_PER_STEP),
        in_specs=[
            pl.BlockSpec((1, 1, q_rows, HEAD_GROUP), lambda b, g, i: (b, g, i, 0)),
            kv_spec, kv_spec,
            meta_spec, meta_spec, meta_spec,
            pl.BlockSpec((N_VARIANTS, PAIRS_PER_GROUP, 2 * BLOCK_Q, ATTN_KEYS),
                         lambda b, g, i: (0, g, 0, 0)),
        ],
        out_specs=[
            pl.BlockSpec((1, q_rows, HEAD_GROUP), lambda b, g, i: (b, i, g)),
            pl.BlockSpec((1, N_META, HEAD_GROUP), lambda b, g, i: (b, 0, g)),
        ],
        out_shape=[
            jax.ShapeDtypeStruct((B, T, D_ATTN), BF16),
            jax.ShapeDtypeStruct((B, N_META, D_ATTN), BF16),
        ],
        compiler_params=_params(3),
        name="attention",
    )(q, k, v, qm, km, vm, bias_tab)


def _bias_table(rpb, meta_bias):
    c = np.arange(GRID_W)
    kc = np.arange(GRID_W)
    cs = np.clip(c - KW // 2, 0, GRID_W - KW)
    valid = (kc[None, :] >= cs[:, None]) & (kc[None, :] < cs[:, None] + KW)
    dc = kc[None, :] - c[:, None] + (KW - 1)
    onehot = ((dc[None] == np.arange(2 * KW - 1)[:, None, None]) & valid[None]).astype(np.float32)
    toep = jnp.einsum('hrj,jck->hrck', rpb.astype(F32) * LOG2_E, onehot,
                      precision=lax.Precision.HIGHEST)
    toep = jnp.where(valid[None, None], toep, MASK_VALUE)
    toep = jnp.pad(toep, ((0, 0), (1, 1), (0, 0), (0, 0)), constant_values=MASK_VALUE)
    pairs = jnp.concatenate([toep[:, :-1], toep[:, 1:]], axis=-1)
    meta = jnp.pad(meta_bias.astype(F32) * LOG2_E, ((0, 0), (GRID_W, GRID_W - N_META)),
                   constant_values=MASK_VALUE).reshape(N_HEADS, 1, PAIR)
    heads_per_step = BIAS_HEADS_PER_STEP
    tab = pl.pallas_call(
        functools.partial(_bias_body, heads=heads_per_step),
        grid=(N_VARIANTS, N_HEADS // heads_per_step),
        in_specs=[
            pl.BlockSpec((heads_per_step, 2 * KH, GRID_W, PAIR), lambda v, g: (g, 0, 0, 0)),
            pl.BlockSpec((heads_per_step, 1, PAIR), lambda v, g: (g, 0, 0)),
        ],
        out_specs=pl.BlockSpec((1, heads_per_step, BLOCK_Q, ATTN_KEYS), lambda v, g: (v, g, 0, 0)),
        out_shape=jax.ShapeDtypeStruct((N_VARIANTS, N_HEADS, BLOCK_Q, ATTN_KEYS), F32),
        compiler_params=_params(2),
        name="bias_table",
    )(pairs, meta)
    return tab.reshape(N_VARIANTS, N_HEADS // 2, 2 * BLOCK_Q, ATTN_KEYS)


def _bias_body(pairs_ref, meta_ref, out_ref, *, heads):
    v = pl.program_id(0)
    offset = jnp.where(v == 0, 0, jnp.where(v == 1, KH // 2, KEY_ROWS - ROWS_PER_BLOCK))
    left = lax.broadcasted_iota(jnp.int32, (GRID_W, PAIR), 1) < GRID_W
    for h in range(heads):
        meta_tile = jnp.broadcast_to(meta_ref[h], (GRID_W, PAIR))
        for ri in range(ROWS_PER_BLOCK):
            lo = jnp.where(v == 0, 0, jnp.where(v == 1, ri, KH // 2 - 1))
            for j in range(ATTN_KEYS // PAIR):
                kr = 2 * j
                dr = kr - offset - ri + (KH - 1)
                tile = pairs_ref[h, jnp.clip(dr + 1, 0, 2 * KH - 1)]
                in_left = jnp.logical_and(kr >= lo, kr < lo + KH)
                in_right = jnp.logical_and(kr + 1 >= lo, kr + 1 < lo + KH)
                if kr + 1 < KEY_ROWS:
                    right_tile = jnp.where(in_right, tile, MASK_VALUE)
                else:
                    right_tile = meta_tile
                tile = jnp.where(left, jnp.where(in_left, tile, MASK_VALUE), right_tile)
                out_ref[0, h, ri * GRID_W:(ri + 1) * GRID_W, j * PAIR:(j + 1) * PAIR] = tile


def _pool_mixer(ext_ref, sum_ref, r0, n, p0, seq_len, wp_ref, ps_ref):
    p = p0 + lax.broadcasted_iota(jnp.int32, (n, 1), 0)
    outs = []
    for g, w in enumerate(POOL_WINDOWS):
        cs = slice(g * POOL_GROUP_DIM, (g + 1) * POOL_GROUP_DIM)
        span = n + 2 * HALO
        s = ext_ref[pl.ds(r0, span), cs]
        width = 1
        while width < w:
            s = s + pltpu.roll(s, span - width, axis=0)
            width *= 2
        sum_ref[:, cs] = s
        cnt = (jnp.minimum(p + w // 2, seq_len) - jnp.maximum(p - w // 2, 0)).astype(F32)
        m = sum_ref[pl.ds(HALO - w // 2, n), cs] / cnt - ext_ref[pl.ds(r0 + HALO, n), cs]
        y = jnp.dot(m.astype(BF16), wp_ref[g], preferred_element_type=F32) * ps_ref[:, cs]
        outs.append(y.astype(BF16))
    return jnp.concatenate(outs, axis=1)


def _mix_body(u_ref, up_ref, un_ref, um_ref, ya_ref, yam_ref, h_ref, hm_ref,
              wp_ref, ps_ref, wo_ref, g_ref, b_ref, h1_ref, h1m_ref,
              ext_ref, sum_ref, extm_ref, summ_ref, cat_ref, *, tile, seq_len):
    i = pl.program_id(1)
    last = pl.num_programs(1) - 1
    um = um_ref[0]
    prev = jnp.where(i == 0, um[N_META - HALO:], up_ref[0])
    nxt = jnp.where(i == last, jnp.zeros((HALO, D_POOL), F32), un_ref[0])
    ext_ref[...] = jnp.concatenate([prev, u_ref[0], nxt], axis=0)
    sub = tile // MIX_SUBTILES
    for r in range(MIX_SUBTILES):
        rows = slice(r * sub, (r + 1) * sub)
        cat_ref[rows, :D_POOL] = _pool_mixer(
            ext_ref, sum_ref.at[r], r * sub, sub, N_META + i * tile + r * sub, seq_len,
            wp_ref, ps_ref)
        cat_ref[rows, D_POOL:] = ya_ref[0, rows, :]
    for r in range(MIX_SUBTILES):
        rows = slice(r * sub, (r + 1) * sub)
        mix = jnp.dot(cat_ref[rows, :], wo_ref[...], preferred_element_type=F32)
        h1_ref[0, rows, :] = _layernorm(ALPHA * h_ref[0, rows, :] + mix, g_ref[...], b_ref[...])

    @pl.when(i == 0)
    def _():
        extm_ref[...] = jnp.concatenate(
            [jnp.zeros((HALO, D_POOL), F32), um, u_ref[0, :HALO, :]], axis=0)
        y_pool_m = _pool_mixer(extm_ref, summ_ref, 0, N_META, 0, seq_len, wp_ref, ps_ref)
        cat_m = jnp.concatenate([y_pool_m, yam_ref[0]], axis=1)
        mix_m = jnp.dot(cat_m, wo_ref[...], preferred_element_type=F32)
        h1m_ref[0] = _layernorm(ALPHA * hm_ref[0] + mix_m, g_ref[...], b_ref[...])


def _halo_specs(tile, n_tokens, width):
    per_tile = tile // HALO
    n_blocks = n_tokens // HALO
    prev = pl.BlockSpec((1, HALO, width),
                        lambda b, i, *_: (b, jnp.maximum(i * per_tile - 1, 0), 0))
    nxt = pl.BlockSpec((1, HALO, width),
                       lambda b, i, *_: (b, jnp.minimum((i + 1) * per_tile, n_blocks - 1), 0))
    return prev, nxt


def _mix(u, um, ya, yam, h, hm, w_pool_bf16, pool_scale, w_out_bf16, ln_g, ln_b, tile):
    B, T, D = h.shape
    prev_spec, next_spec = _halo_specs(tile, T, D_POOL)
    meta = lambda width: pl.BlockSpec((1, N_META, width), lambda b, i: (0, 0, 0))
    meta_b = lambda width: pl.BlockSpec((1, N_META, width), lambda b, i: (b, 0, 0))
    return pl.pallas_call(
        functools.partial(_mix_body, tile=tile, seq_len=N_META + T),
        grid=(B, T // tile),
        in_specs=[
            pl.BlockSpec((1, tile, D_POOL), lambda b, i: (b, i, 0)),
            prev_spec, next_spec,
            meta(D_POOL),
            pl.BlockSpec((1, tile, D_ATTN), lambda b, i: (b, i, 0)),
            meta_b(D_ATTN),
            pl.BlockSpec((1, tile, D), lambda b, i: (b, i, 0)),
            meta(D),
            _const_spec(w_pool_bf16.shape),
            _const_spec((1, D_POOL)),
            _const_spec((D, D)),
            _const_spec((1, D)),
            _const_spec((1, D)),
        ],
        out_specs=[
            pl.BlockSpec((1, tile, D), lambda b, i: (b, i, 0)),
            meta_b(D),
        ],
        out_shape=[
            jax.ShapeDtypeStruct((B, T, D), F32),
            jax.ShapeDtypeStruct((B, N_META, D), F32),
        ],
        scratch_shapes=[
            pltpu.VMEM((tile + 2 * HALO, D_POOL), F32),
            pltpu.VMEM((MIX_SUBTILES, tile // MIX_SUBTILES + 2 * HALO, D_POOL), F32),
            pltpu.VMEM((N_META + 2 * HALO, D_POOL), F32),
            pltpu.VMEM((N_META + 2 * HALO, D_POOL), F32),
            pltpu.VMEM((tile, D), BF16),
        ],
        compiler_params=_params(2),
        name="mix",
    )(u, u, u, um, ya, yam, h, hm, w_pool_bf16, pool_scale, w_out_bf16, ln_g, ln_b)


def _ffn_body(h1_ref, hp_ref, hn_ref, h1m_ref, wa_ref, wg_ref, par_ref, wd_ref,
              g_ref, b_ref, o_ref, lhs_ref, z_ref, acc_ref, *, tile):
    i = pl.program_id(1)
    c = pl.program_id(2)
    last_tile = pl.num_programs(1) - 1
    last_chunk = pl.num_programs(2) - 1
    n_sub = FF_CHUNK // FF_SUB

    @pl.when(c == 0)
    def _():
        prev = jnp.where(i == 0, h1m_ref[0, N_META - HALO:, :], hp_ref[0])
        lhs_ref[...] = jnp.concatenate([prev, h1_ref[0], hn_ref[0]], axis=0).astype(BF16)
        acc_ref[...] = jnp.zeros_like(acc_ref)

    is_seq_end = i == last_tile
    sqrt_half = np.sqrt(0.5).astype(np.float32)

    def conv(half, s, row0, n_rows):
        cs = slice(s * FF_SUB, (s + 1) * FF_SUB)
        z = z_ref.at[half * n_sub + s]
        taps = [par_ref[half, j:j + 1, cs] for j in range(3)]
        const = (taps[0] + taps[1] + taps[2]) * par_ref[half, 4:5, cs] + par_ref[half, 3:4, cs]
        return (z[pl.ds(HALO - 1 + row0, n_rows), :] * taps[0]
                + z[pl.ds(HALO + row0, n_rows), :] * taps[1]
                + z[pl.ds(HALO + 1 + row0, n_rows), :] * taps[2] + const)

    def gated(s, row0, n_rows):
        a = conv(0, s, row0, n_rows)
        g = conv(1, s, row0, n_rows)
        return (a * (0.5 * g * (1.0 + lax.erf(g * sqrt_half)))).astype(BF16)

    lhs = lhs_ref[...]
    for s in range(n_sub):
        cs = slice(s * FF_SUB, (s + 1) * FF_SUB)
        for half, w_ref in enumerate((wa_ref, wg_ref)):
            z = jnp.dot(lhs, w_ref[:, cs], preferred_element_type=F32)
            z_ref[half * n_sub + s] = z
            z_ref[half * n_sub + s, HALO + tile:, :] = jnp.where(
                is_seq_end, -par_ref[half, 4:5, cs], z[HALO + tile:, :])

    down = None
    for s in range(n_sub - 1):
        part = jnp.dot(gated(s, 0, tile), wd_ref[s * FF_SUB:(s + 1) * FF_SUB, :],
                       preferred_element_type=F32)
        down = part if down is None else down + part
    half_rows = tile // 2
    wd_last = wd_ref[(n_sub - 1) * FF_SUB:, :]
    for r in range(2):
        rows = slice(r * half_rows, (r + 1) * half_rows)
        part = jnp.dot(gated(n_sub - 1, r * half_rows, half_rows), wd_last,
                       preferred_element_type=F32)
        acc_ref[rows, :] += part if down is None else down[rows] + part

    @pl.when(c == last_chunk)
    def _():
        o_ref[0] = _layernorm(ALPHA * h1_ref[0] + acc_ref[...], g_ref[...], b_ref[...])


def _ffn(h1, h1m, w_up_p, ff_par, w_down_p, ln_g, ln_b, tile):
    B, T, D = h1.shape
    prev_spec, next_spec = _halo_specs(tile, T, D)
    const3 = lambda shape: pl.BlockSpec(shape, lambda b, i, c: (0,) * len(shape),
                                        pipeline_mode=pl.Buffered(1))
    n_par = ff_par.shape[1]
    return pl.pallas_call(
        functools.partial(_ffn_body, tile=tile),
        grid=(B, T // tile, N_FF_CHUNKS),
        in_specs=[
            pl.BlockSpec((1, tile, D), lambda b, i, c: (b, i, 0)),
            prev_spec, next_spec,
            pl.BlockSpec((1, N_META, D), lambda b, i, c: (b, 0, 0)),
            pl.BlockSpec((D, FF_CHUNK), lambda b, i, c: (0, c)),
            pl.BlockSpec((D, FF_CHUNK), lambda b, i, c: (0, N_FF_CHUNKS + c)),
            pl.BlockSpec((2, n_par, FF_CHUNK), lambda b, i, c: (0, 0, c)),
            pl.BlockSpec((FF_CHUNK, D), lambda b, i, c: (c, 0)),
            const3((1, D)), const3((1, D)),
        ],
        out_specs=pl.BlockSpec((1, tile, D), lambda b, i, c: (b, i, 0)),
        out_shape=jax.ShapeDtypeStruct((B, T, D), F32),
        scratch_shapes=[
            pltpu.VMEM((tile + 2 * HALO, D), BF16),
            pltpu.VMEM((2 * FF_CHUNK // FF_SUB, tile + 2 * HALO, FF_SUB), F32),
            pltpu.VMEM((tile, D), F32),
        ],
        compiler_params=_params(3),
        name="ffn",
    )(h1, h1, h1, h1m, w_up_p, w_up_p, ff_par, w_down_p, ln_g, ln_b)


def _pad_ff_cols(a):
    zeros = jnp.zeros(a.shape[:-1] + (D_FF_PAD - D_FF,), a.dtype)
    return jnp.concatenate([a[..., :D_FF], zeros, a[..., D_FF:], zeros], axis=-1)


def _w_up_body(w_ref, o_ref):
    zeros = jnp.zeros((w_ref.shape[0], D_FF_PAD - D_FF), BF16)
    o_ref[:, :D_FF] = w_ref[:, :D_FF].astype(BF16)
    o_ref[:, D_FF:D_FF_PAD] = zeros
    o_ref[:, D_FF_PAD:D_FF_PAD + D_FF] = w_ref[:, D_FF:].astype(BF16)
    o_ref[:, D_FF_PAD + D_FF:] = zeros


def _prep_w_up(w_up):
    D = w_up.shape[0]
    rows = PREP_ROWS
    return pl.pallas_call(
        _w_up_body,
        grid=(D // rows,),
        in_specs=[pl.BlockSpec((rows, 2 * D_FF), lambda i: (i, 0))],
        out_specs=pl.BlockSpec((rows, 2 * D_FF_PAD), lambda i: (i, 0)),
        out_shape=jax.ShapeDtypeStruct((D, 2 * D_FF_PAD), BF16),
        compiler_params=_params(1),
        name="prep_w_up",
    )(w_up)


def _split_halves(a):
    return jnp.transpose(a.reshape(a.shape[0], 2, D_FF_PAD), (1, 0, 2))


def _encode(x, meta, consts):
    (ln_in_g, ln_in_b, w_in, w_pool, pool_scale, bias_tab, w_out, ln1_g, ln1_b,
     w_up, ff_par, w_down, ln2_g, ln2_b) = consts
    hm, um, qm, km, vm = meta
    h, u, q, k, v = _inproj(x, ln_in_g, ln_in_b, w_in, TOKEN_TILE, INPROJ_SUBTILES)
    ya, yam = _attention(q, k, v, qm, km, vm, bias_tab)
    h1, h1m = _mix(u, um, ya, yam, h, hm, w_pool, pool_scale, w_out, ln1_g, ln1_b, TOKEN_TILE)
    return _ffn(h1, h1m, w_up, ff_par, w_down, ln2_g, ln2_b, TOKEN_TILE)


def kernel(x_prompt, x_sample, meta_tokens, ln_in_g, ln_in_b, w_in, w_pool, pool_scale, rpb,
           meta_bias, w_out, ln1_g, ln1_b, w_up, b_up, conv_w, conv_b, w_down, ln2_g, ln2_b):
    row = lambda a: a.reshape(1, -1).astype(F32)
    bias_tab = _bias_table(rpb[0], meta_bias[0])
    w_down_p = jnp.concatenate(
        [w_down[0].astype(BF16), jnp.zeros((D_FF_PAD - D_FF, D_MODEL), BF16)], axis=0)
    consts = (
        row(ln_in_g), row(ln_in_b), w_in[0].astype(BF16), w_pool[0].astype(BF16),
        row(pool_scale[0]), bias_tab, w_out[0].astype(BF16), row(ln1_g[0]), row(ln1_b[0]),
        _prep_w_up(w_up[0].astype(F32)),
        _split_halves(_pad_ff_cols(jnp.concatenate(
            [conv_w[0].astype(F32), row(conv_b[0]), row(b_up[0])], axis=0))),
        w_down_p, row(ln2_g[0]), row(ln2_b[0]),
    )
    meta = _inproj(meta_tokens[None].astype(F32), consts[0], consts[1], consts[2], N_META, 1)
    y_prompt = _encode(x_prompt, meta, consts)
    y_sample = _encode(x_sample, meta, consts)
    return (y_prompt, y_sample)
```

```python
import functools

import numpy as np
import jax
import jax.numpy as jnp
from jax import lax
from jax.experimental import pallas as pl
from jax.experimental.pallas import tpu as pltpu

F32 = jnp.float32
BF16 = jnp.bfloat16

D_MODEL = 2048
N_META = 16
GRID_W = 64
KH = 8
KW = 16
D_POOL = D_MODEL // 2
D_ATTN = D_MODEL - D_POOL
POOL_WINDOWS = (2, 4, 8, 16)
POOL_GROUP_DIM = D_POOL // len(POOL_WINDOWS)
HEAD_DIM = 64
N_HEADS = D_ATTN // HEAD_DIM
D_FF = 5504
LN_EPS = 1e-5
DEPTH = 1
ALPHA = float((2 * DEPTH) ** 0.25)
QK_SCALE = HEAD_DIM ** -0.5
LOG2_E = float(np.log2(np.e))
MASK_VALUE = -1e30

SUBLANES = 8
PAIR = 2 * HEAD_DIM
HEAD_GROUP = 256
N_HEAD_GROUPS = D_ATTN // HEAD_GROUP
PAIRS_PER_GROUP = HEAD_GROUP // PAIR
HALO = SUBLANES
VMEM_LIMIT_BYTES = 56 * 1024 * 1024

TOKEN_TILE = 512
MIX_SUBTILES = 2
INPROJ_SUBTILES = 2
ROWS_PER_STEP = 32
ROWS_PER_BLOCK = 4
KEY_ROWS = ROWS_PER_BLOCK + KH - 1
ATTN_KEYS = (KEY_ROWS + 1) * GRID_W
BLOCK_Q = ROWS_PER_BLOCK * GRID_W
MIN_GRID_ROWS = 4 * KH


def _block_variants():
    rows = MIN_GRID_ROWS

    def describe(r0):
        ks = min(max(r0 - KH // 2, 0), rows - KEY_ROWS)
        lo = tuple(min(max(r0 + ri - KH // 2, 0), rows - KH) - ks for ri in range(ROWS_PER_BLOCK))
        return (r0 - ks, lo)

    variants = [describe(rows // 2)]
    at_start, at_end = {}, {}
    for r0 in range(0, rows, ROWS_PER_BLOCK):
        d = describe(r0)
        if d == variants[0]:
            continue
        if d not in variants:
            variants.append(d)
        if r0 < rows // 2:
            at_start[r0] = variants.index(d)
        else:
            at_end[rows - r0] = variants.index(d)
    return variants, at_start, at_end


VARIANTS, VARIANT_AT_START, VARIANT_AT_END = _block_variants()
N_VARIANTS = len(VARIANTS)
BIAS_HEADS_PER_STEP = 8
PREP_ROWS = 256
FF_CHUNK = 512
D_FF_PAD = -(-D_FF // FF_CHUNK) * FF_CHUNK
N_FF_CHUNKS = D_FF_PAD // FF_CHUNK
FF_SUB = 256

def _layernorm(x, g, b):
    mu = jnp.mean(x, axis=-1, keepdims=True)
    xc = x - mu
    var = jnp.mean(xc * xc, axis=-1, keepdims=True)
    return xc * lax.rsqrt(var + LN_EPS) * g + b


def _const_spec(shape):
    zeros = (0,) * len(shape)
    return pl.BlockSpec(shape, lambda *_: zeros, pipeline_mode=pl.Buffered(1))


def _params(n_axes):
    return pltpu.CompilerParams(
        dimension_semantics=("arbitrary",) * n_axes,
        vmem_limit_bytes=VMEM_LIMIT_BYTES,
    )


def _inproj_body(x_ref, g_ref, b_ref, w_ref, h_ref, u_ref, q_ref, k_ref, v_ref, hb_ref,
                 *, subtiles):
    sub = x_ref.shape[1] // subtiles

    def norm(r):
        rows = slice(r * sub, (r + 1) * sub)
        h = _layernorm(x_ref[0, rows, :], g_ref[...], b_ref[...])
        h_ref[0, rows, :] = h
        hb_ref[rows, :] = h.astype(BF16)

    def pool_proj(r):
        rows = slice(r * sub, (r + 1) * sub)
        u_ref[0, rows, :] = jnp.dot(hb_ref[rows, :], w_ref[:, 0:D_POOL],
                                    preferred_element_type=F32)

    def qkv_proj(r):
        rows = slice(r * sub, (r + 1) * sub)
        hb = hb_ref[rows, :]
        for j in range(N_HEAD_GROUPS):
            lo = D_POOL + j * HEAD_GROUP
            q = jnp.dot(hb, w_ref[:, lo:lo + HEAD_GROUP], preferred_element_type=F32)
            q_ref[0, j, rows, :] = (q * (QK_SCALE * LOG2_E)).astype(BF16)
            lo += D_ATTN
            k_ref[0, j, rows, :] = jnp.dot(hb, w_ref[:, lo:lo + HEAD_GROUP],
                                           preferred_element_type=F32).astype(BF16)
            lo += D_ATTN
            v_ref[0, j, rows, :] = jnp.dot(hb, w_ref[:, lo:lo + HEAD_GROUP],
                                           preferred_element_type=F32).astype(BF16)

    norm(0)
    for r in range(subtiles):
        pool_proj(r)
        if r + 1 < subtiles:
            norm(r + 1)
        qkv_proj(r)


def _inproj(x, ln_g, ln_b, w_in_bf16, tile, subtiles):
    B, T, D = x.shape
    n_in = w_in_bf16.shape[1]
    hg_shape = (B, N_HEAD_GROUPS, T, HEAD_GROUP)
    hg_spec = pl.BlockSpec((1, N_HEAD_GROUPS, tile, HEAD_GROUP), lambda b, i: (b, 0, i, 0))
    return pl.pallas_call(
        functools.partial(_inproj_body, subtiles=subtiles),
        grid=(B, T // tile),
        in_specs=[
            pl.BlockSpec((1, tile, D), lambda b, i: (b, i, 0)),
            _const_spec((1, D)),
            _const_spec((1, D)),
            _const_spec((D, n_in)),
        ],
        out_specs=[
            pl.BlockSpec((1, tile, D), lambda b, i: (b, i, 0)),
            pl.BlockSpec((1, tile, D_POOL), lambda b, i: (b, i, 0)),
            hg_spec, hg_spec, hg_spec,
        ],
        out_shape=[
            jax.ShapeDtypeStruct((B, T, D), F32),
            jax.ShapeDtypeStruct((B, T, D_POOL), F32),
            jax.ShapeDtypeStruct(hg_shape, BF16),
            jax.ShapeDtypeStruct(hg_shape, BF16),
            jax.ShapeDtypeStruct(hg_shape, BF16),
        ],
        scratch_shapes=[pltpu.VMEM((tile, D), BF16)],
        compiler_params=_params(2),
        name="inproj",
    )(x, ln_g, ln_b, w_in_bf16)


def _attend(q2, keys, vals, bias):
    m = q2.shape[0]
    first = lax.broadcasted_iota(jnp.int32, (m, PAIR), 1) < HEAD_DIM
    zero = jnp.zeros_like(q2)
    qs = jnp.concatenate([jnp.where(first, q2, zero), jnp.where(first, zero, q2)], axis=0)
    s = lax.dot_general(qs, keys, (((1,), (1,)), ((), ())), preferred_element_type=F32) + bias
    e = jnp.exp2(s - jnp.max(s, axis=-1, keepdims=True)).astype(BF16)
    vals_ones = jnp.concatenate([vals, jnp.ones_like(vals)], axis=1)
    o = jnp.dot(e, vals_ones, preferred_element_type=F32)
    o = o[:, :PAIR] / o[:, PAIR:]
    return jnp.where(first, o[:m], o[m:])


def _attn_body(q_ref, k_ref, v_ref, qm_ref, km_ref, vm_ref, bias_ref, y_ref, ym_ref, *, rows):
    rb = pl.program_id(2)
    n_real = KEY_ROWS * GRID_W
    pad = jnp.zeros((GRID_W - N_META, PAIR), BF16)

    def key_set(ref, meta_ref, k_off, ls):
        return jnp.concatenate(
            [ref[0, 0, pl.ds(k_off, n_real), ls], meta_ref[0, 0, :, ls], pad], axis=0)

    for blk in range(ROWS_PER_STEP // ROWS_PER_BLOCK):
        r0 = rb * ROWS_PER_STEP + blk * ROWS_PER_BLOCK
        ks = jnp.clip(r0 - KH // 2, 0, rows - KEY_ROWS)
        variant = 0
        for start, vid in VARIANT_AT_START.items():
            variant = jnp.where(r0 == start, vid, variant)
        for dist, vid in VARIANT_AT_END.items():
            variant = jnp.where(r0 == rows - dist, vid, variant)
        k_off = pl.multiple_of(ks * GRID_W, GRID_W)
        qs = slice(blk * BLOCK_Q, (blk + 1) * BLOCK_Q)
        for p in range(PAIRS_PER_GROUP):
            ls = slice(p * PAIR, (p + 1) * PAIR)
            out = _attend(q_ref[0, 0, qs, ls], key_set(k_ref, km_ref, k_off, ls),
                          key_set(v_ref, vm_ref, k_off, ls), bias_ref[variant, p])
            y_ref[0, qs, ls] = out.astype(BF16)

    @pl.when(rb == 0)
    def _():
        for p in range(PAIRS_PER_GROUP):
            ls = slice(p * PAIR, (p + 1) * PAIR)
            first = VARIANT_AT_START[0]
            bias = jnp.concatenate(
                [jnp.broadcast_to(bias_ref[first, p, 0:1, :], (N_META, ATTN_KEYS)),
                 jnp.broadcast_to(bias_ref[first, p, BLOCK_Q:BLOCK_Q + 1, :],
                                  (N_META, ATTN_KEYS))], axis=0)
            out = _attend(qm_ref[0, 0, :, ls], key_set(k_ref, km_ref, 0, ls),
                          key_set(v_ref, vm_ref, 0, ls), bias)
            ym_ref[0, :, ls] = out.astype(BF16)


def _attention(q, k, v, qm, km, vm, bias_tab):
    B, _, T, _ = q.shape
    rows = T // GRID_W
    assert rows % ROWS_PER_STEP == 0 and rows >= MIN_GRID_ROWS
    q_rows = ROWS_PER_STEP * GRID_W
    meta_spec = pl.BlockSpec((1, 1, N_META, HEAD_GROUP), lambda b, g, i: (0, g, 0, 0))
    kv_spec = pl.BlockSpec((1, 1, T, HEAD_GROUP), lambda b, g, i: (b, g, 0, 0))
    return pl.pallas_call(
        functools.partial(_attn_body, rows=rows),
        grid=(B, N_HEAD_GROUPS, rows // ROWS_PER_STEP),
        in_specs=[
            pl.BlockSpec((1, 1, q_rows, HEAD_GROUP), lambda b, g, i: (b, g, i, 0)),
            kv_spec, kv_spec,
            meta_spec, meta_spec, meta_spec,
            pl.BlockSpec((N_VARIANTS, PAIRS_PER_GROUP, 2 * BLOCK_Q, ATTN_KEYS),
                         lambda b, g, i: (0, g, 0, 0)),
        ],
        out_specs=[
            pl.BlockSpec((1, q_rows, HEAD_GROUP), lambda b, g, i: (b, i, g)),
            pl.BlockSpec((1, N_META, HEAD_GROUP), lambda b, g, i: (b, 0, g)),
        ],
        out_shape=[
            jax.ShapeDtypeStruct((B, T, D_ATTN), BF16),
            jax.ShapeDtypeStruct((B, N_META, D_ATTN), BF16),
        ],
        compiler_params=_params(3),
        name="attention",
    )(q, k, v, qm, km, vm, bias_tab)


def _bias_table(rpb, meta_bias):
    c = np.arange(GRID_W)
    kc = np.arange(GRID_W)
    cs = np.clip(c - KW // 2, 0, GRID_W - KW)
    valid = (kc[None, :] >= cs[:, None]) & (kc[None, :] < cs[:, None] + KW)
    dc = kc[None, :] - c[:, None] + (KW - 1)
    onehot = ((dc[None] == np.arange(2 * KW - 1)[:, None, None]) & valid[None]).astype(np.float32)
    toep = jnp.einsum('hrj,jck->hrck', rpb.astype(F32) * LOG2_E, onehot,
                      precision=lax.Precision.HIGHEST)
    toep = jnp.where(valid[None, None], toep, MASK_VALUE)
    toep = jnp.pad(toep, ((0, 0), (1, 1), (0, 0), (0, 0)), constant_values=MASK_VALUE)
    pairs = jnp.concatenate([toep[:, :-1], toep[:, 1:]], axis=-1)
    meta = jnp.pad(meta_bias.astype(F32) * LOG2_E, ((0, 0), (GRID_W, GRID_W - N_META)),
                   constant_values=MASK_VALUE).reshape(N_HEADS, 1, PAIR)
    heads_per_step = BIAS_HEADS_PER_STEP
    tab = pl.pallas_call(
        functools.partial(_bias_body, heads=heads_per_step),
        grid=(N_VARIANTS, N_HEADS // heads_per_step),
        in_specs=[
            pl.BlockSpec((heads_per_step, 2 * KH, GRID_W, PAIR), lambda v, g: (g, 0, 0, 0)),
            pl.BlockSpec((heads_per_step, 1, PAIR), lambda v, g: (g, 0, 0)),
        ],
        out_specs=pl.BlockSpec((1, heads_per_step, BLOCK_Q, ATTN_KEYS), lambda v, g: (v, g, 0, 0)),
        out_shape=jax.ShapeDtypeStruct((N_VARIANTS, N_HEADS, BLOCK_Q, ATTN_KEYS), F32),
        compiler_params=_params(2),
        name="bias_table",
    )(pairs, meta)
    return tab.reshape(N_VARIANTS, N_HEADS // 2, 2 * BLOCK_Q, ATTN_KEYS)


def _bias_body(pairs_ref, meta_ref, out_ref, *, heads):
    v = pl.program_id(0)

    def of_variant(values):
        out = values[0]
        for vid in range(1, N_VARIANTS):
            out = jnp.where(v == vid, values[vid], out)
        return out

    offset = of_variant([var[0] for var in VARIANTS])
    left = lax.broadcasted_iota(jnp.int32, (GRID_W, PAIR), 1) < GRID_W
    for h in range(heads):
        meta_tile = jnp.broadcast_to(meta_ref[h], (GRID_W, PAIR))
        for ri in range(ROWS_PER_BLOCK):
            lo = of_variant([var[1][ri] for var in VARIANTS])
            for j in range(ATTN_KEYS // PAIR):
                kr = 2 * j
                dr = kr - offset - ri + (KH - 1)
                tile = pairs_ref[h, jnp.clip(dr + 1, 0, 2 * KH - 1)]
                in_left = jnp.logical_and(kr >= lo, kr < lo + KH)
                in_right = jnp.logical_and(kr + 1 >= lo, kr + 1 < lo + KH)
                if kr + 1 < KEY_ROWS:
                    right_tile = jnp.where(in_right, tile, MASK_VALUE)
                else:
                    right_tile = meta_tile
                tile = jnp.where(left, jnp.where(in_left, tile, MASK_VALUE), right_tile)
                out_ref[0, h, ri * GRID_W:(ri + 1) * GRID_W, j * PAIR:(j + 1) * PAIR] = tile


def _pool_mixer(ext_ref, sum_ref, r0, n, p0, seq_len, wp_ref, ps_ref):
    p = p0 + lax.broadcasted_iota(jnp.int32, (n, 1), 0)
    outs = []
    for g, w in enumerate(POOL_WINDOWS):
        cs = slice(g * POOL_GROUP_DIM, (g + 1) * POOL_GROUP_DIM)
        span = n + 2 * HALO
        s = ext_ref[pl.ds(r0, span), cs]
        width = 1
        while width < w:
            s = s + pltpu.roll(s, span - width, axis=0)
            width *= 2
        sum_ref[:, cs] = s
        cnt = (jnp.minimum(p + w // 2, seq_len) - jnp.maximum(p - w // 2, 0)).astype(F32)
        m = sum_ref[pl.ds(HALO - w // 2, n), cs] / cnt - ext_ref[pl.ds(r0 + HALO, n), cs]
        y = jnp.dot(m.astype(BF16), wp_ref[g], preferred_element_type=F32) * ps_ref[:, cs]
        outs.append(y.astype(BF16))
    return jnp.concatenate(outs, axis=1)


def _mix_body(u_ref, up_ref, un_ref, um_ref, ya_ref, yam_ref, h_ref, hm_ref,
              wp_ref, ps_ref, wo_ref, g_ref, b_ref, h1_ref, h1m_ref,
              ext_ref, sum_ref, extm_ref, summ_ref, cat_ref, *, tile, seq_len):
    i = pl.program_id(1)
    last = pl.num_programs(1) - 1
    um = um_ref[0]
    prev = jnp.where(i == 0, um[N_META - HALO:], up_ref[0])
    nxt = jnp.where(i == last, jnp.zeros((HALO, D_POOL), F32), un_ref[0])
    ext_ref[...] = jnp.concatenate([prev, u_ref[0], nxt], axis=0)
    sub = tile // MIX_SUBTILES
    for r in range(MIX_SUBTILES):
        rows = slice(r * sub, (r + 1) * sub)
        cat_ref[rows, :D_POOL] = _pool_mixer(
            ext_ref, sum_ref.at[r], r * sub, sub, N_META + i * tile + r * sub, seq_len,
            wp_ref, ps_ref)
        cat_ref[rows, D_POOL:] = ya_ref[0, rows, :]
    for r in range(MIX_SUBTILES):
        rows = slice(r * sub, (r + 1) * sub)
        mix = jnp.dot(cat_ref[rows, :], wo_ref[...], preferred_element_type=F32)
        h1_ref[0, rows, :] = _layernorm(ALPHA * h_ref[0, rows, :] + mix, g_ref[...], b_ref[...])

    @pl.when(i == 0)
    def _():
        extm_ref[...] = jnp.concatenate(
            [jnp.zeros((HALO, D_POOL), F32), um, u_ref[0, :HALO, :]], axis=0)
        y_pool_m = _pool_mixer(extm_ref, summ_ref, 0, N_META, 0, seq_len, wp_ref, ps_ref)
        cat_m = jnp.concatenate([y_pool_m, yam_ref[0]], axis=1)
        mix_m = jnp.dot(cat_m, wo_ref[...], preferred_element_type=F32)
        h1m_ref[0] = _layernorm(ALPHA * hm_ref[0] + mix_m, g_ref[...], b_ref[...])


def _halo_specs(tile, n_tokens, width):
    per_tile = tile // HALO
    n_blocks = n_tokens // HALO
    prev = pl.BlockSpec((1, HALO, width),
                        lambda b, i, *_: (b, jnp.maximum(i * per_tile - 1, 0), 0))
    nxt = pl.BlockSpec((1, HALO, width),
                       lambda b, i, *_: (b, jnp.minimum((i + 1) * per_tile, n_blocks - 1), 0))
    return prev, nxt


def _mix(u, um, ya, yam, h, hm, w_pool_bf16, pool_scale, w_out_bf16, ln_g, ln_b, tile):
    B, T, D = h.shape
    prev_spec, next_spec = _halo_specs(tile, T, D_POOL)
    meta = lambda width: pl.BlockSpec((1, N_META, width), lambda b, i: (0, 0, 0))
    meta_b = lambda width: pl.BlockSpec((1, N_META, width), lambda b, i: (b, 0, 0))
    return pl.pallas_call(
        functools.partial(_mix_body, tile=tile, seq_len=N_META + T),
        grid=(B, T // tile),
        in_specs=[
            pl.BlockSpec((1, tile, D_POOL), lambda b, i: (b, i, 0)),
            prev_spec, next_spec,
            meta(D_POOL),
            pl.BlockSpec((1, tile, D_ATTN), lambda b, i: (b, i, 0)),
            meta_b(D_ATTN),
            pl.BlockSpec((1, tile, D), lambda b, i: (b, i, 0)),
            meta(D),
            _const_spec(w_pool_bf16.shape),
            _const_spec((1, D_POOL)),
            _const_spec((D, D)),
            _const_spec((1, D)),
            _const_spec((1, D)),
        ],
        out_specs=[
            pl.BlockSpec((1, tile, D), lambda b, i: (b, i, 0)),
            meta_b(D),
        ],
        out_shape=[
            jax.ShapeDtypeStruct((B, T, D), F32),
            jax.ShapeDtypeStruct((B, N_META, D), F32),
        ],
        scratch_shapes=[
            pltpu.VMEM((tile + 2 * HALO, D_POOL), F32),
            pltpu.VMEM((MIX_SUBTILES, tile // MIX_SUBTILES + 2 * HALO, D_POOL), F32),
            pltpu.VMEM((N_META + 2 * HALO, D_POOL), F32),
            pltpu.VMEM((N_META + 2 * HALO, D_POOL), F32),
            pltpu.VMEM((tile, D), BF16),
        ],
        compiler_params=_params(2),
        name="mix",
    )(u, u, u, um, ya, yam, h, hm, w_pool_bf16, pool_scale, w_out_bf16, ln_g, ln_b)


def _ffn_body(h1_ref, hp_ref, hn_ref, h1m_ref, wa_ref, wg_ref, par_ref, wd_ref,
              g_ref, b_ref, o_ref, lhs_ref, z_ref, acc_ref, *, tile):
    i = pl.program_id(1)
    c = pl.program_id(2)
    last_tile = pl.num_programs(1) - 1
    last_chunk = pl.num_programs(2) - 1
    n_sub = FF_CHUNK // FF_SUB

    @pl.when(c == 0)
    def _():
        prev = jnp.where(i == 0, h1m_ref[0, N_META - HALO:, :], hp_ref[0])
        lhs_ref[...] = jnp.concatenate([prev, h1_ref[0], hn_ref[0]], axis=0).astype(BF16)
        acc_ref[...] = jnp.zeros_like(acc_ref)

    is_seq_end = i == last_tile
    sqrt_half = np.sqrt(0.5).astype(np.float32)

    def conv(half, s, row0, n_rows):
        cs = slice(s * FF_SUB, (s + 1) * FF_SUB)
        z = z_ref.at[half * n_sub + s]
        taps = [par_ref[half, j:j + 1, cs] for j in range(3)]
        const = (taps[0] + taps[1] + taps[2]) * par_ref[half, 4:5, cs] + par_ref[half, 3:4, cs]
        return (z[pl.ds(HALO - 1 + row0, n_rows), :] * taps[0]
                + z[pl.ds(HALO + row0, n_rows), :] * taps[1]
                + z[pl.ds(HALO + 1 + row0, n_rows), :] * taps[2] + const)

    def gated(s, row0, n_rows):
        a = conv(0, s, row0, n_rows)
        g = conv(1, s, row0, n_rows)
        return (a * (0.5 * g * (1.0 + lax.erf(g * sqrt_half)))).astype(BF16)

    lhs = lhs_ref[...]
    for s in range(n_sub):
        cs = slice(s * FF_SUB, (s + 1) * FF_SUB)
        for half, w_ref in enumerate((wa_ref, wg_ref)):
            z = jnp.dot(lhs, w_ref[:, cs], preferred_element_type=F32)
            z_ref[half * n_sub + s] = z
            z_ref[half * n_sub + s, HALO + tile:, :] = jnp.where(
                is_seq_end, -par_ref[half, 4:5, cs], z[HALO + tile:, :])

    down = None
    for s in range(n_sub - 1):
        part = jnp.dot(gated(s, 0, tile), wd_ref[s * FF_SUB:(s + 1) * FF_SUB, :],
                       preferred_element_type=F32)
        down = part if down is None else down + part
    half_rows = tile // 2
    wd_last = wd_ref[(n_sub - 1) * FF_SUB:, :]
    for r in range(2):
        rows = slice(r * half_rows, (r + 1) * half_rows)
        part = jnp.dot(gated(n_sub - 1, r * half_rows, half_rows), wd_last,
                       preferred_element_type=F32)
        acc_ref[rows, :] += part if down is None else down[rows] + part

    @pl.when(c == last_chunk)
    def _():
        o_ref[0] = _layernorm(ALPHA * h1_ref[0] + acc_ref[...], g_ref[...], b_ref[...])


def _ffn(h1, h1m, w_up_p, ff_par, w_down_p, ln_g, ln_b, tile):
    B, T, D = h1.shape
    prev_spec, next_spec = _halo_specs(tile, T, D)
    const3 = lambda shape: pl.BlockSpec(shape, lambda b, i, c: (0,) * len(shape),
                                        pipeline_mode=pl.Buffered(1))
    n_par = ff_par.shape[1]
    return pl.pallas_call(
        functools.partial(_ffn_body, tile=tile),
        grid=(B, T // tile, N_FF_CHUNKS),
        in_specs=[
            pl.BlockSpec((1, tile, D), lambda b, i, c: (b, i, 0)),
            prev_spec, next_spec,
            pl.BlockSpec((1, N_META, D), lambda b, i, c: (b, 0, 0)),
            pl.BlockSpec((D, FF_CHUNK), lambda b, i, c: (0, c)),
            pl.BlockSpec((D, FF_CHUNK), lambda b, i, c: (0, N_FF_CHUNKS + c)),
            pl.BlockSpec((2, n_par, FF_CHUNK), lambda b, i, c: (0, 0, c)),
            pl.BlockSpec((FF_CHUNK, D), lambda b, i, c: (c, 0)),
            const3((1, D)), const3((1, D)),
        ],
        out_specs=pl.BlockSpec((1, tile, D), lambda b, i, c: (b, i, 0)),
        out_shape=jax.ShapeDtypeStruct((B, T, D), F32),
        scratch_shapes=[
            pltpu.VMEM((tile + 2 * HALO, D), BF16),
            pltpu.VMEM((2 * FF_CHUNK // FF_SUB, tile + 2 * HALO, FF_SUB), F32),
            pltpu.VMEM((tile, D), F32),
        ],
        compiler_params=_params(3),
        name="ffn",
    )(h1, h1, h1, h1m, w_up_p, w_up_p, ff_par, w_down_p, ln_g, ln_b)


def _pad_ff_cols(a):
    zeros = jnp.zeros(a.shape[:-1] + (D_FF_PAD - D_FF,), a.dtype)
    return jnp.concatenate([a[..., :D_FF], zeros, a[..., D_FF:], zeros], axis=-1)


def _w_up_body(w_ref, o_ref):
    zeros = jnp.zeros((w_ref.shape[0], D_FF_PAD - D_FF), BF16)
    o_ref[:, :D_FF] = w_ref[:, :D_FF].astype(BF16)
    o_ref[:, D_FF:D_FF_PAD] = zeros
    o_ref[:, D_FF_PAD:D_FF_PAD + D_FF] = w_ref[:, D_FF:].astype(BF16)
    o_ref[:, D_FF_PAD + D_FF:] = zeros


def _prep_w_up(w_up):
    D = w_up.shape[0]
    rows = PREP_ROWS
    return pl.pallas_call(
        _w_up_body,
        grid=(D // rows,),
        in_specs=[pl.BlockSpec((rows, 2 * D_FF), lambda i: (i, 0))],
        out_specs=pl.BlockSpec((rows, 2 * D_FF_PAD), lambda i: (i, 0)),
        out_shape=jax.ShapeDtypeStruct((D, 2 * D_FF_PAD), BF16),
        compiler_params=_params(1),
        name="prep_w_up",
    )(w_up)


def _split_halves(a):
    return jnp.transpose(a.reshape(a.shape[0], 2, D_FF_PAD), (1, 0, 2))


def _encode(x, meta, consts):
    (ln_in_g, ln_in_b, w_in, w_pool, pool_scale, bias_tab, w_out, ln1_g, ln1_b,
     w_up, ff_par, w_down, ln2_g, ln2_b) = consts
    hm, um, qm, km, vm = meta
    h, u, q, k, v = _inproj(x, ln_in_g, ln_in_b, w_in, TOKEN_TILE, INPROJ_SUBTILES)
    ya, yam = _attention(q, k, v, qm, km, vm, bias_tab)
    h1, h1m = _mix(u, um, ya, yam, h, hm, w_pool, pool_scale, w_out, ln1_g, ln1_b, TOKEN_TILE)
    return _ffn(h1, h1m, w_up, ff_par, w_down, ln2_g, ln2_b, TOKEN_TILE)


def kernel(x_prompt, x_sample, meta_tokens, ln_in_g, ln_in_b, w_in, w_pool, pool_scale, rpb,
           meta_bias, w_out, ln1_g, ln1_b, w_up, b_up, conv_w, conv_b, w_down, ln2_g, ln2_b):
    row = lambda a: a.reshape(1, -1).astype(F32)
    bias_tab = _bias_table(rpb[0], meta_bias[0])
    w_down_p = jnp.pad(w_down[0], ((0, D_FF_PAD - D_FF), (0, 0))).astype(BF16)
    consts = (
        row(ln_in_g), row(ln_in_b), w_in[0].astype(BF16), w_pool[0].astype(BF16),
        row(pool_scale[0]), bias_tab, w_out[0].astype(BF16), row(ln1_g[0]), row(ln1_b[0]),
        _prep_w_up(w_up[0].astype(F32)),
        _split_halves(_pad_ff_cols(jnp.concatenate(
            [conv_w[0].astype(F32), row(conv_b[0]), row(b_up[0])], axis=0))),
        w_down_p, row(ln2_g[0]), row(ln2_b[0]),
    )
    meta = _inproj(meta_tokens[None].astype(F32), consts[0], consts[1], consts[2], N_META, 1)
    y_prompt = _encode(x_prompt, meta, consts)
    y_sample = _encode(x_sample, meta, consts)
    return (y_prompt, y_sample)
```

```python
import functools

import numpy as np
import jax
import jax.numpy as jnp
from jax import lax
from jax.experimental import pallas as pl
from jax.experimental.pallas import tpu as pltpu

F32 = jnp.float32
BF16 = jnp.bfloat16

D_MODEL = 2048
N_META = 16
GRID_W = 64
KH = 8
KW = 16
D_POOL = D_MODEL // 2
D_ATTN = D_MODEL - D_POOL
POOL_WINDOWS = (2, 4, 8, 16)
POOL_GROUP_DIM = D_POOL // len(POOL_WINDOWS)
HEAD_DIM = 64
N_HEADS = D_ATTN // HEAD_DIM
D_FF = 5504
LN_EPS = 1e-5
DEPTH = 1
ALPHA = float((2 * DEPTH) ** 0.25)
QK_SCALE = HEAD_DIM ** -0.5
LOG2_E = float(np.log2(np.e))
MASK_VALUE = -1e30

SUBLANES = 8
PAIR = 2 * HEAD_DIM
HEAD_GROUP = 256
N_HEAD_GROUPS = D_ATTN // HEAD_GROUP
PAIRS_PER_GROUP = HEAD_GROUP // PAIR
HALO = SUBLANES
VMEM_LIMIT_BYTES = 56 * 1024 * 1024

TOKEN_TILE = 512
MIX_SUBTILES = 2
INPROJ_SUBTILES = 2
ROWS_PER_STEP = 64
ROWS_PER_BLOCK = 4
KEY_ROWS = ROWS_PER_BLOCK + KH - 1
ATTN_KEYS = (KEY_ROWS + 1) * GRID_W
BLOCK_Q = ROWS_PER_BLOCK * GRID_W
MIN_GRID_ROWS = 4 * KH


def _block_variants():
    rows = MIN_GRID_ROWS

    def describe(r0):
        ks = min(max(r0 - KH // 2, 0), rows - KEY_ROWS)
        lo = tuple(min(max(r0 + ri - KH // 2, 0), rows - KH) - ks for ri in range(ROWS_PER_BLOCK))
        return (r0 - ks, lo)

    variants = [describe(rows // 2)]
    at_start, at_end = {}, {}
    for r0 in range(0, rows, ROWS_PER_BLOCK):
        d = describe(r0)
        if d == variants[0]:
            continue
        if d not in variants:
            variants.append(d)
        if r0 < rows // 2:
            at_start[r0] = variants.index(d)
        else:
            at_end[rows - r0] = variants.index(d)
    return variants, at_start, at_end


VARIANTS, VARIANT_AT_START, VARIANT_AT_END = _block_variants()
N_VARIANTS = len(VARIANTS)
BIAS_HEADS_PER_STEP = 8
PREP_ROWS = 256
FF_CHUNK = 512
D_FF_PAD = -(-D_FF // FF_CHUNK) * FF_CHUNK
N_FF_CHUNKS = D_FF_PAD // FF_CHUNK
FF_SUB = 256

def _layernorm(x, g, b):
    mu = jnp.mean(x, axis=-1, keepdims=True)
    xc = x - mu
    var = jnp.mean(xc * xc, axis=-1, keepdims=True)
    return xc * lax.rsqrt(var + LN_EPS) * g + b


def _const_spec(shape):
    zeros = (0,) * len(shape)
    return pl.BlockSpec(shape, lambda *_: zeros, pipeline_mode=pl.Buffered(1))


def _params(n_axes):
    return pltpu.CompilerParams(
        dimension_semantics=("arbitrary",) * n_axes,
        vmem_limit_bytes=VMEM_LIMIT_BYTES,
    )


def _inproj_body(x_ref, g_ref, b_ref, w_ref, h_ref, u_ref, q_ref, k_ref, v_ref, hb_ref,
                 *, subtiles):
    sub = x_ref.shape[1] // subtiles

    def norm(r):
        rows = slice(r * sub, (r + 1) * sub)
        h = _layernorm(x_ref[0, rows, :], g_ref[...], b_ref[...])
        h_ref[0, rows, :] = h
        hb_ref[rows, :] = h.astype(BF16)

    def pool_proj(r):
        rows = slice(r * sub, (r + 1) * sub)
        u_ref[0, rows, :] = jnp.dot(hb_ref[rows, :], w_ref[:, 0:D_POOL],
                                    preferred_element_type=F32)

    def qkv_proj(r):
        rows = slice(r * sub, (r + 1) * sub)
        hb = hb_ref[rows, :]
        for j in range(N_HEAD_GROUPS):
            lo = D_POOL + j * HEAD_GROUP
            q = jnp.dot(hb, w_ref[:, lo:lo + HEAD_GROUP], preferred_element_type=F32)
            q_ref[0, j, rows, :] = (q * (QK_SCALE * LOG2_E)).astype(BF16)
            lo += D_ATTN
            k_ref[0, j, rows, :] = jnp.dot(hb, w_ref[:, lo:lo + HEAD_GROUP],
                                           preferred_element_type=F32).astype(BF16)
            lo += D_ATTN
            v_ref[0, j, rows, :] = jnp.dot(hb, w_ref[:, lo:lo + HEAD_GROUP],
                                           preferred_element_type=F32).astype(BF16)

    norm(0)
    for r in range(subtiles):
        pool_proj(r)
        if r + 1 < subtiles:
            norm(r + 1)
        qkv_proj(r)


def _inproj(x, ln_g, ln_b, w_in_bf16, tile, subtiles):
    B, T, D = x.shape
    n_in = w_in_bf16.shape[1]
    hg_shape = (B, N_HEAD_GROUPS, T, HEAD_GROUP)
    hg_spec = pl.BlockSpec((1, N_HEAD_GROUPS, tile, HEAD_GROUP), lambda b, i: (b, 0, i, 0))
    return pl.pallas_call(
        functools.partial(_inproj_body, subtiles=subtiles),
        grid=(B, T // tile),
        in_specs=[
            pl.BlockSpec((1, tile, D), lambda b, i: (b, i, 0)),
            _const_spec((1, D)),
            _const_spec((1, D)),
            _const_spec((D, n_in)),
        ],
        out_specs=[
            pl.BlockSpec((1, tile, D), lambda b, i: (b, i, 0)),
            pl.BlockSpec((1, tile, D_POOL), lambda b, i: (b, i, 0)),
            hg_spec, hg_spec, hg_spec,
        ],
        out_shape=[
            jax.ShapeDtypeStruct((B, T, D), F32),
            jax.ShapeDtypeStruct((B, T, D_POOL), F32),
            jax.ShapeDtypeStruct(hg_shape, BF16),
            jax.ShapeDtypeStruct(hg_shape, BF16),
            jax.ShapeDtypeStruct(hg_shape, BF16),
        ],
        scratch_shapes=[pltpu.VMEM((tile, D), BF16)],
        compiler_params=_params(2),
        name="inproj",
    )(x, ln_g, ln_b, w_in_bf16)


def _attend(q2, keys, vals, bias):
    m = q2.shape[0]
    first = lax.broadcasted_iota(jnp.int32, (m, PAIR), 1) < HEAD_DIM
    zero = jnp.zeros_like(q2)
    qs = jnp.concatenate([jnp.where(first, q2, zero), jnp.where(first, zero, q2)], axis=0)
    s = lax.dot_general(qs, keys, (((1,), (1,)), ((), ())), preferred_element_type=F32) + bias
    e = jnp.exp2(s - jnp.max(s, axis=-1, keepdims=True)).astype(BF16)
    vals_ones = jnp.concatenate([vals, jnp.ones_like(vals)], axis=1)
    o = jnp.dot(e, vals_ones, preferred_element_type=F32)
    o = o[:, :PAIR] / o[:, PAIR:]
    return jnp.where(first, o[:m], o[m:])


def _attn_body(q_ref, k_ref, v_ref, qm_ref, km_ref, vm_ref, bias_ref, y_ref, ym_ref, *, rows):
    rb = pl.program_id(2)
    n_real = KEY_ROWS * GRID_W
    pad = jnp.zeros((GRID_W - N_META, PAIR), BF16)

    def key_set(ref, meta_ref, k_off, ls):
        return jnp.concatenate(
            [ref[0, 0, pl.ds(k_off, n_real), ls], meta_ref[0, 0, :, ls], pad], axis=0)

    for blk in range(ROWS_PER_STEP // ROWS_PER_BLOCK):
        r0 = rb * ROWS_PER_STEP + blk * ROWS_PER_BLOCK
        ks = jnp.clip(r0 - KH // 2, 0, rows - KEY_ROWS)
        variant = 0
        for start, vid in VARIANT_AT_START.items():
            variant = jnp.where(r0 == start, vid, variant)
        for dist, vid in VARIANT_AT_END.items():
            variant = jnp.where(r0 == rows - dist, vid, variant)
        k_off = pl.multiple_of(ks * GRID_W, GRID_W)
        qs = slice(blk * BLOCK_Q, (blk + 1) * BLOCK_Q)
        for p in range(PAIRS_PER_GROUP):
            ls = slice(p * PAIR, (p + 1) * PAIR)
            out = _attend(q_ref[0, 0, qs, ls], key_set(k_ref, km_ref, k_off, ls),
                          key_set(v_ref, vm_ref, k_off, ls), bias_ref[variant, p])
            y_ref[0, qs, ls] = out.astype(BF16)

    @pl.when(rb == 0)
    def _():
        for p in range(PAIRS_PER_GROUP):
            ls = slice(p * PAIR, (p + 1) * PAIR)
            first = VARIANT_AT_START[0]
            bias = jnp.concatenate(
                [jnp.broadcast_to(bias_ref[first, p, 0:1, :], (N_META, ATTN_KEYS)),
                 jnp.broadcast_to(bias_ref[first, p, BLOCK_Q:BLOCK_Q + 1, :],
                                  (N_META, ATTN_KEYS))], axis=0)
            out = _attend(qm_ref[0, 0, :, ls], key_set(k_ref, km_ref, 0, ls),
                          key_set(v_ref, vm_ref, 0, ls), bias)
            ym_ref[0, :, ls] = out.astype(BF16)


def _attention(q, k, v, qm, km, vm, bias_tab):
    B, _, T, _ = q.shape
    rows = T // GRID_W
    assert rows % ROWS_PER_STEP == 0 and rows >= MIN_GRID_ROWS
    q_rows = ROWS_PER_STEP * GRID_W
    meta_spec = pl.BlockSpec((1, 1, N_META, HEAD_GROUP), lambda b, g, i: (0, g, 0, 0))
    kv_spec = pl.BlockSpec((1, 1, T, HEAD_GROUP), lambda b, g, i: (b, g, 0, 0))
    return pl.pallas_call(
        functools.partial(_attn_body, rows=rows),
        grid=(B, N_HEAD_GROUPS, rows // ROWS_PER_STEP),
        in_specs=[
            pl.BlockSpec((1, 1, q_rows, HEAD_GROUP), lambda b, g, i: (b, g, i, 0)),
            kv_spec, kv_spec,
            meta_spec, meta_spec, meta_spec,
            pl.BlockSpec((N_VARIANTS, PAIRS_PER_GROUP, 2 * BLOCK_Q, ATTN_KEYS),
                         lambda b, g, i: (0, g, 0, 0)),
        ],
        out_specs=[
            pl.BlockSpec((1, q_rows, HEAD_GROUP), lambda b, g, i: (b, i, g)),
            pl.BlockSpec((1, N_META, HEAD_GROUP), lambda b, g, i: (b, 0, g)),
        ],
        out_shape=[
            jax.ShapeDtypeStruct((B, T, D_ATTN), BF16),
            jax.ShapeDtypeStruct((B, N_META, D_ATTN), BF16),
        ],
        compiler_params=_params(3),
        name="attention",
    )(q, k, v, qm, km, vm, bias_tab)


def _bias_table(rpb, meta_bias):
    c = np.arange(GRID_W)
    kc = np.arange(GRID_W)
    cs = np.clip(c - KW // 2, 0, GRID_W - KW)
    valid = (kc[None, :] >= cs[:, None]) & (kc[None, :] < cs[:, None] + KW)
    dc = kc[None, :] - c[:, None] + (KW - 1)
    onehot = ((dc[None] == np.arange(2 * KW - 1)[:, None, None]) & valid[None]).astype(np.float32)
    toep = jnp.einsum('hrj,jck->hrck', rpb.astype(F32) * LOG2_E, onehot,
                      precision=lax.Precision.HIGHEST)
    toep = jnp.where(valid[None, None], toep, MASK_VALUE)
    toep = jnp.pad(toep, ((0, 0), (1, 1), (0, 0), (0, 0)), constant_values=MASK_VALUE)
    pairs = jnp.concatenate([toep[:, :-1], toep[:, 1:]], axis=-1)
    meta = jnp.pad(meta_bias.astype(F32) * LOG2_E, ((0, 0), (GRID_W, GRID_W - N_META)),
                   constant_values=MASK_VALUE).reshape(N_HEADS, 1, PAIR)
    heads_per_step = BIAS_HEADS_PER_STEP
    tab = pl.pallas_call(
        functools.partial(_bias_body, heads=heads_per_step),
        grid=(N_VARIANTS, N_HEADS // heads_per_step),
        in_specs=[
            pl.BlockSpec((heads_per_step, 2 * KH, GRID_W, PAIR), lambda v, g: (g, 0, 0, 0)),
            pl.BlockSpec((heads_per_step, 1, PAIR), lambda v, g: (g, 0, 0)),
        ],
        out_specs=pl.BlockSpec((1, heads_per_step, BLOCK_Q, ATTN_KEYS), lambda v, g: (v, g, 0, 0)),
        out_shape=jax.ShapeDtypeStruct((N_VARIANTS, N_HEADS, BLOCK_Q, ATTN_KEYS), F32),
        compiler_params=_params(2),
        name="bias_table",
    )(pairs, meta)
    return tab.reshape(N_VARIANTS, N_HEADS // 2, 2 * BLOCK_Q, ATTN_KEYS)


def _bias_body(pairs_ref, meta_ref, out_ref, *, heads):
    v = pl.program_id(0)

    def of_variant(values):
        out = values[0]
        for vid in range(1, N_VARIANTS):
            out = jnp.where(v == vid, values[vid], out)
        return out

    offset = of_variant([var[0] for var in VARIANTS])
    left = lax.broadcasted_iota(jnp.int32, (GRID_W, PAIR), 1) < GRID_W
    for h in range(heads):
        meta_tile = jnp.broadcast_to(meta_ref[h], (GRID_W, PAIR))
        for ri in range(ROWS_PER_BLOCK):
            lo = of_variant([var[1][ri] for var in VARIANTS])
            for j in range(ATTN_KEYS // PAIR):
                kr = 2 * j
                dr = kr - offset - ri + (KH - 1)
                tile = pairs_ref[h, jnp.clip(dr + 1, 0, 2 * KH - 1)]
                in_left = jnp.logical_and(kr >= lo, kr < lo + KH)
                in_right = jnp.logical_and(kr + 1 >= lo, kr + 1 < lo + KH)
                if kr + 1 < KEY_ROWS:
                    right_tile = jnp.where(in_right, tile, MASK_VALUE)
                else:
                    right_tile = meta_tile
                tile = jnp.where(left, jnp.where(in_left, tile, MASK_VALUE), right_tile)
                out_ref[0, h, ri * GRID_W:(ri + 1) * GRID_W, j * PAIR:(j + 1) * PAIR] = tile


def _pool_mixer(ext_ref, sum_ref, r0, n, p0, seq_len, wp_ref, ps_ref):
    p = p0 + lax.broadcasted_iota(jnp.int32, (n, 1), 0)
    outs = []
    for g, w in enumerate(POOL_WINDOWS):
        cs = slice(g * POOL_GROUP_DIM, (g + 1) * POOL_GROUP_DIM)
        span = n + 2 * HALO
        s = ext_ref[pl.ds(r0, span), cs]
        width = 1
        while width < w:
            s = s + pltpu.roll(s, span - width, axis=0)
            width *= 2
        sum_ref[:, cs] = s
        cnt = (jnp.minimum(p + w // 2, seq_len) - jnp.maximum(p - w // 2, 0)).astype(F32)
        m = sum_ref[pl.ds(HALO - w // 2, n), cs] / cnt - ext_ref[pl.ds(r0 + HALO, n), cs]
        y = jnp.dot(m.astype(BF16), wp_ref[g], preferred_element_type=F32) * ps_ref[:, cs]
        outs.append(y.astype(BF16))
    return jnp.concatenate(outs, axis=1)


def _mix_body(u_ref, up_ref, un_ref, um_ref, ya_ref, yam_ref, h_ref, hm_ref,
              wp_ref, ps_ref, wo_ref, g_ref, b_ref, h1_ref, h1m_ref,
              ext_ref, sum_ref, extm_ref, summ_ref, cat_ref, *, tile, seq_len):
    i = pl.program_id(1)
    last = pl.num_programs(1) - 1
    um = um_ref[0]
    prev = jnp.where(i == 0, um[N_META - HALO:], up_ref[0])
    nxt = jnp.where(i == last, jnp.zeros((HALO, D_POOL), F32), un_ref[0])
    ext_ref[...] = jnp.concatenate([prev, u_ref[0], nxt], axis=0)
    sub = tile // MIX_SUBTILES
    for r in range(MIX_SUBTILES):
        rows = slice(r * sub, (r + 1) * sub)
        cat_ref[rows, :D_POOL] = _pool_mixer(
            ext_ref, sum_ref.at[r], r * sub, sub, N_META + i * tile + r * sub, seq_len,
            wp_ref, ps_ref)
        cat_ref[rows, D_POOL:] = ya_ref[0, rows, :]
    for r in range(MIX_SUBTILES):
        rows = slice(r * sub, (r + 1) * sub)
        mix = jnp.dot(cat_ref[rows, :], wo_ref[...], preferred_element_type=F32)
        h1_ref[0, rows, :] = _layernorm(ALPHA * h_ref[0, rows, :] + mix, g_ref[...], b_ref[...])

    @pl.when(i == 0)
    def _():
        extm_ref[...] = jnp.concatenate(
            [jnp.zeros((HALO, D_POOL), F32), um, u_ref[0, :HALO, :]], axis=0)
        y_pool_m = _pool_mixer(extm_ref, summ_ref, 0, N_META, 0, seq_len, wp_ref, ps_ref)
        cat_m = jnp.concatenate([y_pool_m, yam_ref[0]], axis=1)
        mix_m = jnp.dot(cat_m, wo_ref[...], preferred_element_type=F32)
        h1m_ref[0] = _layernorm(ALPHA * hm_ref[0] + mix_m, g_ref[...], b_ref[...])


def _halo_specs(tile, n_tokens, width):
    per_tile = tile // HALO
    n_blocks = n_tokens // HALO
    prev = pl.BlockSpec((1, HALO, width),
                        lambda b, i, *_: (b, jnp.maximum(i * per_tile - 1, 0), 0))
    nxt = pl.BlockSpec((1, HALO, width),
                       lambda b, i, *_: (b, jnp.minimum((i + 1) * per_tile, n_blocks - 1), 0))
    return prev, nxt


def _mix(u, um, ya, yam, h, hm, w_pool_bf16, pool_scale, w_out_bf16, ln_g, ln_b, tile):
    B, T, D = h.shape
    prev_spec, next_spec = _halo_specs(tile, T, D_POOL)
    meta = lambda width: pl.BlockSpec((1, N_META, width), lambda b, i: (0, 0, 0))
    meta_b = lambda width: pl.BlockSpec((1, N_META, width), lambda b, i: (b, 0, 0))
    return pl.pallas_call(
        functools.partial(_mix_body, tile=tile, seq_len=N_META + T),
        grid=(B, T // tile),
        in_specs=[
            pl.BlockSpec((1, tile, D_POOL), lambda b, i: (b, i, 0)),
            prev_spec, next_spec,
            meta(D_POOL),
            pl.BlockSpec((1, tile, D_ATTN), lambda b, i: (b, i, 0)),
            meta_b(D_ATTN),
            pl.BlockSpec((1, tile, D), lambda b, i: (b, i, 0)),
            meta(D),
            _const_spec(w_pool_bf16.shape),
            _const_spec((1, D_POOL)),
            _const_spec((D, D)),
            _const_spec((1, D)),
            _const_spec((1, D)),
        ],
        out_specs=[
            pl.BlockSpec((1, tile, D), lambda b, i: (b, i, 0)),
            meta_b(D),
        ],
        out_shape=[
            jax.ShapeDtypeStruct((B, T, D), F32),
            jax.ShapeDtypeStruct((B, N_META, D), F32),
        ],
        scratch_shapes=[
            pltpu.VMEM((tile + 2 * HALO, D_POOL), F32),
            pltpu.VMEM((MIX_SUBTILES, tile // MIX_SUBTILES + 2 * HALO, D_POOL), F32),
            pltpu.VMEM((N_META + 2 * HALO, D_POOL), F32),
            pltpu.VMEM((N_META + 2 * HALO, D_POOL), F32),
            pltpu.VMEM((tile, D), BF16),
        ],
        compiler_params=_params(2),
        name="mix",
    )(u, u, u, um, ya, yam, h, hm, w_pool_bf16, pool_scale, w_out_bf16, ln_g, ln_b)


def _ffn_body(h1_ref, hp_ref, hn_ref, h1m_ref, wa_ref, wg_ref, par_ref, wd_ref,
              g_ref, b_ref, o_ref, lhs_ref, z_ref, acc_ref, *, tile):
    i = pl.program_id(1)
    c = pl.program_id(2)
    last_tile = pl.num_programs(1) - 1
    last_chunk = pl.num_programs(2) - 1
    n_sub = FF_CHUNK // FF_SUB

    @pl.when(c == 0)
    def _():
        prev = jnp.where(i == 0, h1m_ref[0, N_META - HALO:, :], hp_ref[0])
        lhs_ref[...] = jnp.concatenate([prev, h1_ref[0], hn_ref[0]], axis=0).astype(BF16)
        acc_ref[...] = jnp.zeros_like(acc_ref)

    is_seq_end = i == last_tile
    sqrt_half = np.sqrt(0.5).astype(np.float32)

    def conv(half, s, row0, n_rows):
        cs = slice(s * FF_SUB, (s + 1) * FF_SUB)
        z = z_ref.at[half * n_sub + s]
        taps = [par_ref[half, j:j + 1, cs] for j in range(3)]
        const = (taps[0] + taps[1] + taps[2]) * par_ref[half, 4:5, cs] + par_ref[half, 3:4, cs]
        return (z[pl.ds(HALO - 1 + row0, n_rows), :] * taps[0]
                + z[pl.ds(HALO + row0, n_rows), :] * taps[1]
                + z[pl.ds(HALO + 1 + row0, n_rows), :] * taps[2] + const)

    def gated(s, row0, n_rows):
        a = conv(0, s, row0, n_rows)
        g = conv(1, s, row0, n_rows)
        return (a * (0.5 * g * (1.0 + lax.erf(g * sqrt_half)))).astype(BF16)

    lhs = lhs_ref[...]
    for s in range(n_sub):
        cs = slice(s * FF_SUB, (s + 1) * FF_SUB)
        for half, w_ref in enumerate((wa_ref, wg_ref)):
            z = jnp.dot(lhs, w_ref[:, cs], preferred_element_type=F32)
            z_ref[half * n_sub + s] = z
            z_ref[half * n_sub + s, HALO + tile:, :] = jnp.where(
                is_seq_end, -par_ref[half, 4:5, cs], z[HALO + tile:, :])

    down = None
    for s in range(n_sub - 1):
        part = jnp.dot(gated(s, 0, tile), wd_ref[s * FF_SUB:(s + 1) * FF_SUB, :],
                       preferred_element_type=F32)
        down = part if down is None else down + part
    half_rows = tile // 2
    wd_last = wd_ref[(n_sub - 1) * FF_SUB:, :]
    for r in range(2):
        rows = slice(r * half_rows, (r + 1) * half_rows)
        part = jnp.dot(gated(n_sub - 1, r * half_rows, half_rows), wd_last,
                       preferred_element_type=F32)
        acc_ref[rows, :] += part if down is None else down[rows] + part

    @pl.when(c == last_chunk)
    def _():
        o_ref[0] = _layernorm(ALPHA * h1_ref[0] + acc_ref[...], g_ref[...], b_ref[...])


def _ffn(h1, h1m, w_up_p, ff_par, w_down_p, ln_g, ln_b, tile):
    B, T, D = h1.shape
    prev_spec, next_spec = _halo_specs(tile, T, D)
    const3 = lambda shape: pl.BlockSpec(shape, lambda b, i, c: (0,) * len(shape),
                                        pipeline_mode=pl.Buffered(1))
    n_par = ff_par.shape[1]
    return pl.pallas_call(
        functools.partial(_ffn_body, tile=tile),
        grid=(B, T // tile, N_FF_CHUNKS),
        in_specs=[
            pl.BlockSpec((1, tile, D), lambda b, i, c: (b, i, 0)),
            prev_spec, next_spec,
            pl.BlockSpec((1, N_META, D), lambda b, i, c: (b, 0, 0)),
            pl.BlockSpec((D, FF_CHUNK), lambda b, i, c: (0, c)),
            pl.BlockSpec((D, FF_CHUNK), lambda b, i, c: (0, N_FF_CHUNKS + c)),
            pl.BlockSpec((2, n_par, FF_CHUNK), lambda b, i, c: (0, 0, c)),
            pl.BlockSpec((FF_CHUNK, D), lambda b, i, c: (c, 0)),
            const3((1, D)), const3((1, D)),
        ],
        out_specs=pl.BlockSpec((1, tile, D), lambda b, i, c: (b, i, 0)),
        out_shape=jax.ShapeDtypeStruct((B, T, D), F32),
        scratch_shapes=[
            pltpu.VMEM((tile + 2 * HALO, D), BF16),
            pltpu.VMEM((2 * FF_CHUNK // FF_SUB, tile + 2 * HALO, FF_SUB), F32),
            pltpu.VMEM((tile, D), F32),
        ],
        compiler_params=_params(3),
        name="ffn",
    )(h1, h1, h1, h1m, w_up_p, w_up_p, ff_par, w_down_p, ln_g, ln_b)


def _pad_ff_cols(a):
    zeros = jnp.zeros(a.shape[:-1] + (D_FF_PAD - D_FF,), a.dtype)
    return jnp.concatenate([a[..., :D_FF], zeros, a[..., D_FF:], zeros], axis=-1)


def _w_up_body(w_ref, o_ref):
    zeros = jnp.zeros((w_ref.shape[0], D_FF_PAD - D_FF), BF16)
    o_ref[:, :D_FF] = w_ref[:, :D_FF].astype(BF16)
    o_ref[:, D_FF:D_FF_PAD] = zeros
    o_ref[:, D_FF_PAD:D_FF_PAD + D_FF] = w_ref[:, D_FF:].astype(BF16)
    o_ref[:, D_FF_PAD + D_FF:] = zeros


def _prep_w_up(w_up):
    D = w_up.shape[0]
    rows = PREP_ROWS
    return pl.pallas_call(
        _w_up_body,
        grid=(D // rows,),
        in_specs=[pl.BlockSpec((rows, 2 * D_FF), lambda i: (i, 0))],
        out_specs=pl.BlockSpec((rows, 2 * D_FF_PAD), lambda i: (i, 0)),
        out_shape=jax.ShapeDtypeStruct((D, 2 * D_FF_PAD), BF16),
        compiler_params=_params(1),
        name="prep_w_up",
    )(w_up)


def _split_halves(a):
    return jnp.transpose(a.reshape(a.shape[0], 2, D_FF_PAD), (1, 0, 2))


def _encode(x, meta, consts):
    (ln_in_g, ln_in_b, w_in, w_pool, pool_scale, bias_tab, w_out, ln1_g, ln1_b,
     w_up, ff_par, w_down, ln2_g, ln2_b) = consts
    hm, um, qm, km, vm = meta
    h, u, q, k, v = _inproj(x, ln_in_g, ln_in_b, w_in, TOKEN_TILE, INPROJ_SUBTILES)
    ya, yam = _attention(q, k, v, qm, km, vm, bias_tab)
    h1, h1m = _mix(u, um, ya, yam, h, hm, w_pool, pool_scale, w_out, ln1_g, ln1_b, TOKEN_TILE)
    return _ffn(h1, h1m, w_up, ff_par, w_down, ln2_g, ln2_b, TOKEN_TILE)


def kernel(x_prompt, x_sample, meta_tokens, ln_in_g, ln_in_b, w_in, w_pool, pool_scale, rpb,
           meta_bias, w_out, ln1_g, ln1_b, w_up, b_up, conv_w, conv_b, w_down, ln2_g, ln2_b):
    row = lambda a: a.reshape(1, -1).astype(F32)
    bias_tab = _bias_table(rpb[0], meta_bias[0])
    w_down_p = jnp.pad(w_down[0], ((0, D_FF_PAD - D_FF), (0, 0))).astype(BF16)
    consts = (
        row(ln_in_g), row(ln_in_b), w_in[0].astype(BF16), w_pool[0].astype(BF16),
        row(pool_scale[0]), bias_tab, w_out[0].astype(BF16), row(ln1_g[0]), row(ln1_b[0]),
        _prep_w_up(w_up[0].astype(F32)),
        _split_halves(_pad_ff_cols(jnp.concatenate(
            [conv_w[0].astype(F32), row(conv_b[0]), row(b_up[0])], axis=0))),
        w_down_p, row(ln2_g[0]), row(ln2_b[0]),
    )
    meta = _inproj(meta_tokens[None].astype(F32), consts[0], consts[1], consts[2], N_META, 1)
    y_prompt = _encode(x_prompt, meta, consts)
    y_sample = _encode(x_sample, meta, consts)
    return (y_prompt, y_sample)
```

```python
import functools

import numpy as np
import jax
import jax.numpy as jnp
from jax import lax
from jax.experimental import pallas as pl
from jax.experimental.pallas import tpu as pltpu

F32 = jnp.float32
BF16 = jnp.bfloat16

D_MODEL = 2048
N_META = 16
GRID_W = 64
KH = 8
KW = 16
D_POOL = D_MODEL // 2
D_ATTN = D_MODEL - D_POOL
POOL_WINDOWS = (2, 4, 8, 16)
POOL_GROUP_DIM = D_POOL // len(POOL_WINDOWS)
HEAD_DIM = 64
N_HEADS = D_ATTN // HEAD_DIM
D_FF = 5504
LN_EPS = 1e-5
DEPTH = 1
ALPHA = float((2 * DEPTH) ** 0.25)
QK_SCALE = HEAD_DIM ** -0.5
LOG2_E = float(np.log2(np.e))
MASK_VALUE = -1e30

LANES = 128
SUBLANES = 8
PAIR = 2 * HEAD_DIM
HEAD_GROUP = 256
N_HEAD_GROUPS = D_ATTN // HEAD_GROUP
PAIRS_PER_GROUP = HEAD_GROUP // PAIR
HALO = SUBLANES
VMEM_LIMIT_BYTES = 56 * 1024 * 1024

TOKEN_TILE = 512
MIX_SUBTILES = 2
INPROJ_SUBTILES = 2
ROWS_PER_STEP = 64
ROWS_PER_BLOCK = 4
KEY_ROWS = ROWS_PER_BLOCK + KH - 1
ATTN_KEYS = (KEY_ROWS + 1) * GRID_W
BLOCK_Q = ROWS_PER_BLOCK * GRID_W
N_VARIANTS = 3
FF_CHUNK = 512
D_FF_PAD = -(-D_FF // FF_CHUNK) * FF_CHUNK
N_FF_CHUNKS = D_FF_PAD // FF_CHUNK
FF_SUB = 256


def _layernorm(x, g, b):
    mu = jnp.mean(x, axis=-1, keepdims=True)
    xc = x - mu
    var = jnp.mean(xc * xc, axis=-1, keepdims=True)
    return xc * lax.rsqrt(var + LN_EPS) * g + b


def _const_spec(shape):
    zeros = (0,) * len(shape)
    return pl.BlockSpec(shape, lambda *_: zeros, pipeline_mode=pl.Buffered(1))


def _params(n_axes, flags=None):
    return pltpu.CompilerParams(
        dimension_semantics=("arbitrary",) * n_axes,
        vmem_limit_bytes=VMEM_LIMIT_BYTES,
        flags=flags,
    )


def _inproj_body(x_ref, g_ref, b_ref, w_ref, h_ref, u_ref, q_ref, k_ref, v_ref, hb_ref,
                 *, subtiles):
    sub = x_ref.shape[1] // subtiles

    def norm(r):
        rows = slice(r * sub, (r + 1) * sub)
        h = _layernorm(x_ref[0, rows, :], g_ref[...], b_ref[...])
        h_ref[0, rows, :] = h
        hb_ref[rows, :] = h.astype(BF16)

    def pool_proj(r):
        rows = slice(r * sub, (r + 1) * sub)
        u_ref[0, rows, :] = jnp.dot(hb_ref[rows, :], w_ref[:, 0:D_POOL],
                                    preferred_element_type=F32)

    def qkv_proj(r):
        rows = slice(r * sub, (r + 1) * sub)
        hb = hb_ref[rows, :]
        for j in range(N_HEAD_GROUPS):
            lo = D_POOL + j * HEAD_GROUP
            q = jnp.dot(hb, w_ref[:, lo:lo + HEAD_GROUP], preferred_element_type=F32)
            q_ref[0, j, rows, :] = (q * (QK_SCALE * LOG2_E)).astype(BF16)
            lo += D_ATTN
            k_ref[0, j, rows, :] = jnp.dot(hb, w_ref[:, lo:lo + HEAD_GROUP],
                                           preferred_element_type=F32).astype(BF16)
            lo += D_ATTN
            v_ref[0, j, rows, :] = jnp.dot(hb, w_ref[:, lo:lo + HEAD_GROUP],
                                           preferred_element_type=F32).astype(BF16)

    norm(0)
    for r in range(subtiles):
        pool_proj(r)
        if r + 1 < subtiles:
            norm(r + 1)
        qkv_proj(r)


def _inproj(x, ln_g, ln_b, w_in_bf16, tile, subtiles):
    B, T, D = x.shape
    n_in = w_in_bf16.shape[1]
    hg_shape = (B, N_HEAD_GROUPS, T, HEAD_GROUP)
    hg_spec = pl.BlockSpec((1, N_HEAD_GROUPS, tile, HEAD_GROUP), lambda b, i: (b, 0, i, 0))
    return pl.pallas_call(
        functools.partial(_inproj_body, subtiles=subtiles),
        grid=(B, T // tile),
        in_specs=[
            pl.BlockSpec((1, tile, D), lambda b, i: (b, i, 0)),
            _const_spec((1, D)),
            _const_spec((1, D)),
            _const_spec((D, n_in)),
        ],
        out_specs=[
            pl.BlockSpec((1, tile, D), lambda b, i: (b, i, 0)),
            pl.BlockSpec((1, tile, D_POOL), lambda b, i: (b, i, 0)),
            hg_spec, hg_spec, hg_spec,
        ],
        out_shape=[
            jax.ShapeDtypeStruct((B, T, D), F32),
            jax.ShapeDtypeStruct((B, T, D_POOL), F32),
            jax.ShapeDtypeStruct(hg_shape, BF16),
            jax.ShapeDtypeStruct(hg_shape, BF16),
            jax.ShapeDtypeStruct(hg_shape, BF16),
        ],
        scratch_shapes=[pltpu.VMEM((tile, D), BF16)],
        compiler_params=_params(2),
        name="inproj",
    )(x, ln_g, ln_b, w_in_bf16)


def _attend(q2, keys, vals, bias):
    m = q2.shape[0]
    first = lax.broadcasted_iota(jnp.int32, (m, PAIR), 1) < HEAD_DIM
    zero = jnp.zeros_like(q2)
    qs = jnp.concatenate([jnp.where(first, q2, zero), jnp.where(first, zero, q2)], axis=0)
    s = lax.dot_general(qs, keys, (((1,), (1,)), ((), ())), preferred_element_type=F32) + bias
    e = jnp.exp2(s - jnp.max(s, axis=-1, keepdims=True)).astype(BF16)
    vals_ones = jnp.concatenate([vals, jnp.ones_like(vals)], axis=1)
    o = jnp.dot(e, vals_ones, preferred_element_type=F32)
    o = o[:, :PAIR] / o[:, PAIR:]
    return jnp.where(first, o[:m], o[m:])


def _attn_body(q_ref, k_ref, v_ref, qm_ref, km_ref, vm_ref, bias_ref, y_ref, ym_ref, *, rows):
    rb = pl.program_id(2)
    n_real = KEY_ROWS * GRID_W
    pad = jnp.zeros((GRID_W - N_META, PAIR), BF16)

    def key_set(ref, meta_ref, k_off, ls):
        return jnp.concatenate(
            [ref[0, 0, pl.ds(k_off, n_real), ls], meta_ref[0, 0, :, ls], pad], axis=0)

    for blk in range(ROWS_PER_STEP // ROWS_PER_BLOCK):
        r0 = rb * ROWS_PER_STEP + blk * ROWS_PER_BLOCK
        ks = jnp.clip(r0 - KH // 2, 0, rows - KEY_ROWS)
        variant = jnp.where(r0 == 0, 0, jnp.where(r0 == rows - ROWS_PER_BLOCK, 2, 1))
        k_off = pl.multiple_of(ks * GRID_W, GRID_W)
        qs = slice(blk * BLOCK_Q, (blk + 1) * BLOCK_Q)
        for p in range(PAIRS_PER_GROUP):
            ls = slice(p * PAIR, (p + 1) * PAIR)
            out = _attend(q_ref[0, 0, qs, ls], key_set(k_ref, km_ref, k_off, ls),
                          key_set(v_ref, vm_ref, k_off, ls), bias_ref[variant, p])
            y_ref[0, qs, ls] = out.astype(BF16)

    @pl.when(rb == 0)
    def _():
        for p in range(PAIRS_PER_GROUP):
            ls = slice(p * PAIR, (p + 1) * PAIR)
            bias = jnp.concatenate(
                [jnp.broadcast_to(bias_ref[0, p, 0:1, :], (N_META, ATTN_KEYS)),
                 jnp.broadcast_to(bias_ref[0, p, BLOCK_Q:BLOCK_Q + 1, :], (N_META, ATTN_KEYS))],
                axis=0)
            out = _attend(qm_ref[0, 0, :, ls], key_set(k_ref, km_ref, 0, ls),
                          key_set(v_ref, vm_ref, 0, ls), bias)
            ym_ref[0, :, ls] = out.astype(BF16)


def _attention(q, k, v, qm, km, vm, bias_tab):
    B, _, T, _ = q.shape
    rows = T // GRID_W
    assert rows % ROWS_PER_STEP == 0 and rows >= KEY_ROWS + 1
    q_rows = ROWS_PER_STEP * GRID_W
    meta_spec = pl.BlockSpec((1, 1, N_META, HEAD_GROUP), lambda b, g, i: (0, g, 0, 0))
    kv_spec = pl.BlockSpec((1, 1, T, HEAD_GROUP), lambda b, g, i: (b, g, 0, 0))
    return pl.pallas_call(
        functools.partial(_attn_body, rows=rows),
        grid=(B, N_HEAD_GROUPS, rows // ROWS_PER_STEP),
        in_specs=[
            pl.BlockSpec((1, 1, q_rows, HEAD_GROUP), lambda b, g, i: (b, g, i, 0)),
            kv_spec, kv_spec,
            meta_spec, meta_spec, meta_spec,
            pl.BlockSpec((N_VARIANTS, PAIRS_PER_GROUP, 2 * BLOCK_Q, ATTN_KEYS),
                         lambda b, g, i: (0, g, 0, 0)),
        ],
        out_specs=[
            pl.BlockSpec((1, q_rows, HEAD_GROUP), lambda b, g, i: (b, i, g)),
            pl.BlockSpec((1, N_META, HEAD_GROUP), lambda b, g, i: (b, 0, g)),
        ],
        out_shape=[
            jax.ShapeDtypeStruct((B, T, D_ATTN), BF16),
            jax.ShapeDtypeStruct((B, N_META, D_ATTN), BF16),
        ],
        compiler_params=_params(3),
        name="attention",
    )(q, k, v, qm, km, vm, bias_tab)


def _bias_table(rpb, meta_bias):
    c = np.arange(GRID_W)
    kc = np.arange(GRID_W)
    cs = np.clip(c - KW // 2, 0, GRID_W - KW)
    valid = (kc[None, :] >= cs[:, None]) & (kc[None, :] < cs[:, None] + KW)
    dc = kc[None, :] - c[:, None] + (KW - 1)
    onehot = ((dc[None] == np.arange(2 * KW - 1)[:, None, None]) & valid[None]).astype(np.float32)
    toep = jnp.einsum('hrj,jck->hrck', rpb.astype(F32) * LOG2_E, onehot,
                      precision=lax.Precision.HIGHEST)
    toep = jnp.where(valid[None, None], toep, MASK_VALUE)
    toep = jnp.pad(toep, ((0, 0), (1, 1), (0, 0), (0, 0)), constant_values=MASK_VALUE)
    pairs = jnp.concatenate([toep[:, :-1], toep[:, 1:]], axis=-1)
    meta = jnp.pad(meta_bias.astype(F32) * LOG2_E, ((0, 0), (GRID_W, GRID_W - N_META)),
                   constant_values=MASK_VALUE).reshape(N_HEADS, 1, PAIR)
    heads_per_step = 4
    tab = pl.pallas_call(
        functools.partial(_bias_body, heads=heads_per_step),
        grid=(N_VARIANTS, N_HEADS // heads_per_step),
        in_specs=[
            pl.BlockSpec((heads_per_step, 2 * KH, GRID_W, PAIR), lambda v, g: (g, 0, 0, 0)),
            pl.BlockSpec((heads_per_step, 1, PAIR), lambda v, g: (g, 0, 0)),
        ],
        out_specs=pl.BlockSpec((1, heads_per_step, BLOCK_Q, ATTN_KEYS), lambda v, g: (v, g, 0, 0)),
        out_shape=jax.ShapeDtypeStruct((N_VARIANTS, N_HEADS, BLOCK_Q, ATTN_KEYS), F32),
        compiler_params=_params(2),
        name="bias_table",
    )(pairs, meta)
    return tab.reshape(N_VARIANTS, N_HEADS // 2, 2 * BLOCK_Q, ATTN_KEYS)


def _bias_body(pairs_ref, meta_ref, out_ref, *, heads):
    v = pl.program_id(0)
    offset = jnp.where(v == 0, 0, jnp.where(v == 1, KH // 2, KEY_ROWS - ROWS_PER_BLOCK))
    left = lax.broadcasted_iota(jnp.int32, (GRID_W, PAIR), 1) < GRID_W
    for h in range(heads):
        meta_tile = jnp.broadcast_to(meta_ref[h], (GRID_W, PAIR))
        for ri in range(ROWS_PER_BLOCK):
            lo = jnp.where(v == 0, 0, jnp.where(v == 1, ri, KH // 2 - 1))
            for j in range(ATTN_KEYS // PAIR):
                kr = 2 * j
                dr = kr - offset - ri + (KH - 1)
                tile = pairs_ref[h, jnp.clip(dr + 1, 0, 2 * KH - 1)]
                in_left = jnp.logical_and(kr >= lo, kr < lo + KH)
                in_right = jnp.logical_and(kr + 1 >= lo, kr + 1 < lo + KH)
                if kr + 1 < KEY_ROWS:
                    right_tile = jnp.where(in_right, tile, MASK_VALUE)
                else:
                    right_tile = meta_tile
                tile = jnp.where(left, jnp.where(in_left, tile, MASK_VALUE), right_tile)
                out_ref[0, h, ri * GRID_W:(ri + 1) * GRID_W, j * PAIR:(j + 1) * PAIR] = tile


def _pool_mixer(ext_ref, sum_ref, r0, n, p0, seq_len, wp_ref, ps_ref):
    p = p0 + lax.broadcasted_iota(jnp.int32, (n, 1), 0)
    outs = []
    for g, w in enumerate(POOL_WINDOWS):
        cs = slice(g * POOL_GROUP_DIM, (g + 1) * POOL_GROUP_DIM)
        span = n + 2 * HALO
        s = ext_ref[pl.ds(r0, span), cs]
        width = 1
        while width < w:
            s = s + pltpu.roll(s, span - width, axis=0)
            width *= 2
        sum_ref[:, cs] = s
        cnt = (jnp.minimum(p + w // 2, seq_len) - jnp.maximum(p - w // 2, 0)).astype(F32)
        m = sum_ref[pl.ds(HALO - w // 2, n), cs] / cnt - ext_ref[pl.ds(r0 + HALO, n), cs]
        y = jnp.dot(m.astype(BF16), wp_ref[g], preferred_element_type=F32) * ps_ref[:, cs]
        outs.append(y.astype(BF16))
    return jnp.concatenate(outs, axis=1)


def _mix_body(u_ref, up_ref, un_ref, um_ref, ya_ref, yam_ref, h_ref, hm_ref,
              wp_ref, ps_ref, wo_ref, g_ref, b_ref, h1_ref, h1m_ref,
              ext_ref, sum_ref, extm_ref, summ_ref, cat_ref, *, tile, seq_len):
    i = pl.program_id(1)
    last = pl.num_programs(1) - 1
    um = um_ref[0]
    prev = jnp.where(i == 0, um[N_META - HALO:], up_ref[0])
    nxt = jnp.where(i == last, jnp.zeros((HALO, D_POOL), F32), un_ref[0])
    ext_ref[...] = jnp.concatenate([prev, u_ref[0], nxt], axis=0)
    sub = tile // MIX_SUBTILES
    for r in range(MIX_SUBTILES):
        rows = slice(r * sub, (r + 1) * sub)
        cat_ref[rows, :D_POOL] = _pool_mixer(
            ext_ref, sum_ref.at[r], r * sub, sub, N_META + i * tile + r * sub, seq_len,
            wp_ref, ps_ref)
        cat_ref[rows, D_POOL:] = ya_ref[0, rows, :]
    for r in range(MIX_SUBTILES):
        rows = slice(r * sub, (r + 1) * sub)
        mix = jnp.dot(cat_ref[rows, :], wo_ref[...], preferred_element_type=F32)
        h1_ref[0, rows, :] = _layernorm(ALPHA * h_ref[0, rows, :] + mix, g_ref[...], b_ref[...])

    @pl.when(i == 0)
    def _():
        extm_ref[...] = jnp.concatenate(
            [jnp.zeros((HALO, D_POOL), F32), um, u_ref[0, :HALO, :]], axis=0)
        y_pool_m = _pool_mixer(extm_ref, summ_ref, 0, N_META, 0, seq_len, wp_ref, ps_ref)
        cat_m = jnp.concatenate([y_pool_m, yam_ref[0]], axis=1)
        mix_m = jnp.dot(cat_m, wo_ref[...], preferred_element_type=F32)
        h1m_ref[0] = _layernorm(ALPHA * hm_ref[0] + mix_m, g_ref[...], b_ref[...])


def _halo_specs(tile, n_tokens, width):
    per_tile = tile // HALO
    n_blocks = n_tokens // HALO
    prev = pl.BlockSpec((1, HALO, width),
                        lambda b, i, *_: (b, jnp.maximum(i * per_tile - 1, 0), 0))
    nxt = pl.BlockSpec((1, HALO, width),
                       lambda b, i, *_: (b, jnp.minimum((i + 1) * per_tile, n_blocks - 1), 0))
    return prev, nxt


def _mix(u, um, ya, yam, h, hm, w_pool_bf16, pool_scale, w_out_bf16, ln_g, ln_b, tile):
    B, T, D = h.shape
    prev_spec, next_spec = _halo_specs(tile, T, D_POOL)
    meta = lambda width: pl.BlockSpec((1, N_META, width), lambda b, i: (0, 0, 0))
    meta_b = lambda width: pl.BlockSpec((1, N_META, width), lambda b, i: (b, 0, 0))
    return pl.pallas_call(
        functools.partial(_mix_body, tile=tile, seq_len=N_META + T),
        grid=(B, T // tile),
        in_specs=[
            pl.BlockSpec((1, tile, D_POOL), lambda b, i: (b, i, 0)),
            prev_spec, next_spec,
            meta(D_POOL),
            pl.BlockSpec((1, tile, D_ATTN), lambda b, i: (b, i, 0)),
            meta_b(D_ATTN),
            pl.BlockSpec((1, tile, D), lambda b, i: (b, i, 0)),
            meta(D),
            _const_spec(w_pool_bf16.shape),
            _const_spec((1, D_POOL)),
            _const_spec((D, D)),
            _const_spec((1, D)),
            _const_spec((1, D)),
        ],
        out_specs=[
            pl.BlockSpec((1, tile, D), lambda b, i: (b, i, 0)),
            meta_b(D),
        ],
        out_shape=[
            jax.ShapeDtypeStruct((B, T, D), F32),
            jax.ShapeDtypeStruct((B, N_META, D), F32),
        ],
        scratch_shapes=[
            pltpu.VMEM((tile + 2 * HALO, D_POOL), F32),
            pltpu.VMEM((MIX_SUBTILES, tile // MIX_SUBTILES + 2 * HALO, D_POOL), F32),
            pltpu.VMEM((N_META + 2 * HALO, D_POOL), F32),
            pltpu.VMEM((N_META + 2 * HALO, D_POOL), F32),
            pltpu.VMEM((tile, D), BF16),
        ],
        compiler_params=_params(2),
        name="mix",
    )(u, u, u, um, ya, yam, h, hm, w_pool_bf16, pool_scale, w_out_bf16, ln_g, ln_b)


def _ffn_body(h1_ref, hp_ref, hn_ref, h1m_ref, wa_ref, wg_ref, par_ref, wd_ref,
              g_ref, b_ref, o_ref, lhs_ref, z_ref, acc_ref, *, tile):
    i = pl.program_id(1)
    c = pl.program_id(2)
    last_tile = pl.num_programs(1) - 1
    last_chunk = pl.num_programs(2) - 1
    n_sub = FF_CHUNK // FF_SUB

    @pl.when(c == 0)
    def _():
        prev = jnp.where(i == 0, h1m_ref[0, N_META - HALO:, :], hp_ref[0])
        lhs_ref[...] = jnp.concatenate([prev, h1_ref[0], hn_ref[0]], axis=0).astype(BF16)
        acc_ref[...] = jnp.zeros_like(acc_ref)

    is_seq_end = i == last_tile
    sqrt_half = np.sqrt(0.5).astype(np.float32)

    def conv(half, s, row0, n_rows):
        cs = slice(s * FF_SUB, (s + 1) * FF_SUB)
        z = z_ref.at[half * n_sub + s]
        taps = [par_ref[half, j:j + 1, cs] for j in range(3)]
        const = (taps[0] + taps[1] + taps[2]) * par_ref[half, 4:5, cs] + par_ref[half, 3:4, cs]
        return (z[pl.ds(HALO - 1 + row0, n_rows), :] * taps[0]
                + z[pl.ds(HALO + row0, n_rows), :] * taps[1]
                + z[pl.ds(HALO + 1 + row0, n_rows), :] * taps[2] + const)

    def gated(s, row0, n_rows):
        a = conv(0, s, row0, n_rows)
        g = conv(1, s, row0, n_rows)
        return (a * (0.5 * g * (1.0 + lax.erf(g * sqrt_half)))).astype(BF16)

    lhs = lhs_ref[...]
    for s in range(n_sub):
        cs = slice(s * FF_SUB, (s + 1) * FF_SUB)
        for half, w_ref in enumerate((wa_ref, wg_ref)):
            z = jnp.dot(lhs, w_ref[:, cs], preferred_element_type=F32)
            z_ref[half * n_sub + s] = z
            z_ref[half * n_sub + s, HALO + tile:, :] = jnp.where(
                is_seq_end, -par_ref[half, 4:5, cs], z[HALO + tile:, :])

    down = None
    for s in range(n_sub - 1):
        part = jnp.dot(gated(s, 0, tile), wd_ref[s * FF_SUB:(s + 1) * FF_SUB, :],
                       preferred_element_type=F32)
        down = part if down is None else down + part
    half_rows = tile // 2
    wd_last = wd_ref[(n_sub - 1) * FF_SUB:, :]
    for r in range(2):
        rows = slice(r * half_rows, (r + 1) * half_rows)
        part = jnp.dot(gated(n_sub - 1, r * half_rows, half_rows), wd_last,
                       preferred_element_type=F32)
        acc_ref[rows, :] += part if down is None else down[rows] + part

    @pl.when(c == last_chunk)
    def _():
        o_ref[0] = _layernorm(ALPHA * h1_ref[0] + acc_ref[...], g_ref[...], b_ref[...])


def _ffn(h1, h1m, w_up_p, ff_par, w_down_p, ln_g, ln_b, tile):
    B, T, D = h1.shape
    prev_spec, next_spec = _halo_specs(tile, T, D)
    const3 = lambda shape: pl.BlockSpec(shape, lambda b, i, c: (0,) * len(shape),
                                        pipeline_mode=pl.Buffered(1))
    n_par = ff_par.shape[1]
    return pl.pallas_call(
        functools.partial(_ffn_body, tile=tile),
        grid=(B, T // tile, N_FF_CHUNKS),
        in_specs=[
            pl.BlockSpec((1, tile, D), lambda b, i, c: (b, i, 0)),
            prev_spec, next_spec,
            pl.BlockSpec((1, N_META, D), lambda b, i, c: (b, 0, 0)),
            pl.BlockSpec((D, FF_CHUNK), lambda b, i, c: (0, c)),
            pl.BlockSpec((D, FF_CHUNK), lambda b, i, c: (0, N_FF_CHUNKS + c)),
            pl.BlockSpec((2, n_par, FF_CHUNK), lambda b, i, c: (0, 0, c)),
            pl.BlockSpec((FF_CHUNK, D), lambda b, i, c: (c, 0)),
            const3((1, D)), const3((1, D)),
        ],
        out_specs=pl.BlockSpec((1, tile, D), lambda b, i, c: (b, i, 0)),
        out_shape=jax.ShapeDtypeStruct((B, T, D), F32),
        scratch_shapes=[
            pltpu.VMEM((tile + 2 * HALO, D), BF16),
            pltpu.VMEM((2 * FF_CHUNK // FF_SUB, tile + 2 * HALO, FF_SUB), F32),
            pltpu.VMEM((tile, D), F32),
        ],
        compiler_params=_params(3),
        name="ffn",
    )(h1, h1, h1, h1m, w_up_p, w_up_p, ff_par, w_down_p, ln_g, ln_b)


def _pad_ff_cols(a):
    zeros = jnp.zeros(a.shape[:-1] + (D_FF_PAD - D_FF,), a.dtype)
    return jnp.concatenate([a[..., :D_FF], zeros, a[..., D_FF:], zeros], axis=-1)


def _w_up_body(w_ref, o_ref):
    zeros = jnp.zeros((w_ref.shape[0], D_FF_PAD - D_FF), BF16)
    o_ref[:, :D_FF] = w_ref[:, :D_FF].astype(BF16)
    o_ref[:, D_FF:D_FF_PAD] = zeros
    o_ref[:, D_FF_PAD:D_FF_PAD + D_FF] = w_ref[:, D_FF:].astype(BF16)
    o_ref[:, D_FF_PAD + D_FF:] = zeros


def _prep_w_up(w_up):
    D = w_up.shape[0]
    rows = 128
    return pl.pallas_call(
        _w_up_body,
        grid=(D // rows,),
        in_specs=[pl.BlockSpec((rows, 2 * D_FF), lambda i: (i, 0))],
        out_specs=pl.BlockSpec((rows, 2 * D_FF_PAD), lambda i: (i, 0)),
        out_shape=jax.ShapeDtypeStruct((D, 2 * D_FF_PAD), BF16),
        compiler_params=_params(1),
        name="prep_w_up",
    )(w_up)


def _split_halves(a):
    return jnp.transpose(a.reshape(a.shape[0], 2, D_FF_PAD), (1, 0, 2))


def _encode(x, meta, consts):
    (ln_in_g, ln_in_b, w_in, w_pool, pool_scale, bias_tab, w_out, ln1_g, ln1_b,
     w_up, ff_par, w_down, ln2_g, ln2_b) = consts
    hm, um, qm, km, vm = meta
    h, u, q, k, v = _inproj(x, ln_in_g, ln_in_b, w_in, TOKEN_TILE, INPROJ_SUBTILES)
    ya, yam = _attention(q, k, v, qm, km, vm, bias_tab)
    h1, h1m = _mix(u, um, ya, yam, h, hm, w_pool, pool_scale, w_out, ln1_g, ln1_b, TOKEN_TILE)
    return _ffn(h1, h1m, w_up, ff_par, w_down, ln2_g, ln2_b, TOKEN_TILE)


def kernel(x_prompt, x_sample, meta_tokens, ln_in_g, ln_in_b, w_in, w_pool, pool_scale, rpb,
           meta_bias, w_out, ln1_g, ln1_b, w_up, b_up, conv_w, conv_b, w_down, ln2_g, ln2_b):
    row = lambda a: a.reshape(1, -1).astype(F32)
    bias_tab = _bias_table(rpb[0], meta_bias[0])
    w_down_p = jnp.concatenate(
        [w_down[0].astype(BF16), jnp.zeros((D_FF_PAD - D_FF, D_MODEL), BF16)], axis=0)
    consts = (
        row(ln_in_g), row(ln_in_b), w_in[0].astype(BF16), w_pool[0].astype(BF16),
        row(pool_scale[0]), bias_tab, w_out[0].astype(BF16), row(ln1_g[0]), row(ln1_b[0]),
        _prep_w_up(w_up[0].astype(F32)),
        _split_halves(_pad_ff_cols(jnp.concatenate(
            [conv_w[0].astype(F32), row(conv_b[0]), row(b_up[0])], axis=0))),
        w_down_p, row(ln2_g[0]), row(ln2_b[0]),
    )
    meta = _inproj(meta_tokens[None].astype(F32), consts[0], consts[1], consts[2], N_META, 1)
    y_prompt = _encode(x_prompt, meta, consts)
    y_sample = _encode(x_sample, meta, consts)
    return (y_prompt, y_sample)
```

```python
import functools

import numpy as np
import jax
import jax.numpy as jnp
from jax import lax
from jax.experimental import pallas as pl
from jax.experimental.pallas import tpu as pltpu

F32 = jnp.float32
BF16 = jnp.bfloat16

D_MODEL = 2048
N_META = 16
GRID_W = 64
KH = 8
KW = 16
D_POOL = D_MODEL // 2
D_ATTN = D_MODEL - D_POOL
POOL_WINDOWS = (2, 4, 8, 16)
POOL_GROUP_DIM = D_POOL // len(POOL_WINDOWS)
HEAD_DIM = 64
N_HEADS = D_ATTN // HEAD_DIM
D_FF = 5504
LN_EPS = 1e-5
DEPTH = 1
ALPHA = float((2 * DEPTH) ** 0.25)
QK_SCALE = HEAD_DIM ** -0.5
LOG2_E = float(np.log2(np.e))
MASK_VALUE = -1e30

LANES = 128
SUBLANES = 8
PAIR = 2 * HEAD_DIM
HEAD_GROUP = 256
N_HEAD_GROUPS = D_ATTN // HEAD_GROUP
PAIRS_PER_GROUP = HEAD_GROUP // PAIR
HALO = SUBLANES
VMEM_LIMIT_BYTES = 56 * 1024 * 1024

TOKEN_TILE = 512
MIX_SUBTILES = 2
INPROJ_SUBTILES = 2
ROWS_PER_STEP = 64
ROWS_PER_BLOCK = 4
KEY_ROWS = ROWS_PER_BLOCK + KH - 1
ATTN_KEYS = (KEY_ROWS + 1) * GRID_W
BLOCK_Q = ROWS_PER_BLOCK * GRID_W
N_VARIANTS = 3
FF_CHUNK = 512
D_FF_PAD = -(-D_FF // FF_CHUNK) * FF_CHUNK
N_FF_CHUNKS = D_FF_PAD // FF_CHUNK
FF_SUB = 256


def _layernorm(x, g, b):
    mu = jnp.mean(x, axis=-1, keepdims=True)
    xc = x - mu
    var = jnp.mean(xc * xc, axis=-1, keepdims=True)
    return xc * lax.rsqrt(var + LN_EPS) * g + b


def _const_spec(shape):
    zeros = (0,) * len(shape)
    return pl.BlockSpec(shape, lambda *_: zeros, pipeline_mode=pl.Buffered(1))


def _params(n_axes, flags=None):
    return pltpu.CompilerParams(
        dimension_semantics=("arbitrary",) * n_axes,
        vmem_limit_bytes=VMEM_LIMIT_BYTES,
        flags=flags,
    )


def _inproj_body(x_ref, g_ref, b_ref, w_ref, h_ref, u_ref, q_ref, k_ref, v_ref, hb_ref,
                 *, subtiles):
    sub = x_ref.shape[1] // subtiles

    def norm(r):
        rows = slice(r * sub, (r + 1) * sub)
        h = _layernorm(x_ref[0, rows, :], g_ref[...], b_ref[...])
        h_ref[0, rows, :] = h
        hb_ref[rows, :] = h.astype(BF16)

    def pool_proj(r):
        rows = slice(r * sub, (r + 1) * sub)
        u_ref[0, rows, :] = jnp.dot(hb_ref[rows, :], w_ref[:, 0:D_POOL],
                                    preferred_element_type=F32)

    def qkv_proj(r):
        rows = slice(r * sub, (r + 1) * sub)
        hb = hb_ref[rows, :]
        for j in range(N_HEAD_GROUPS):
            lo = D_POOL + j * HEAD_GROUP
            q = jnp.dot(hb, w_ref[:, lo:lo + HEAD_GROUP], preferred_element_type=F32)
            q_ref[0, j, rows, :] = (q * (QK_SCALE * LOG2_E)).astype(BF16)
            lo += D_ATTN
            k_ref[0, j, rows, :] = jnp.dot(hb, w_ref[:, lo:lo + HEAD_GROUP],
                                           preferred_element_type=F32).astype(BF16)
            lo += D_ATTN
            v_ref[0, j, rows, :] = jnp.dot(hb, w_ref[:, lo:lo + HEAD_GROUP],
                                           preferred_element_type=F32).astype(BF16)

    norm(0)
    for r in range(subtiles):
        pool_proj(r)
        if r + 1 < subtiles:
            norm(r + 1)
        qkv_proj(r)


def _inproj(x, ln_g, ln_b, w_in_bf16, tile, subtiles):
    B, T, D = x.shape
    n_in = w_in_bf16.shape[1]
    hg_shape = (B, N_HEAD_GROUPS, T, HEAD_GROUP)
    hg_spec = pl.BlockSpec((1, N_HEAD_GROUPS, tile, HEAD_GROUP), lambda b, i: (b, 0, i, 0))
    return pl.pallas_call(
        functools.partial(_inproj_body, subtiles=subtiles),
        grid=(B, T // tile),
        in_specs=[
            pl.BlockSpec((1, tile, D), lambda b, i: (b, i, 0)),
            _const_spec((1, D)),
            _const_spec((1, D)),
            _const_spec((D, n_in)),
        ],
        out_specs=[
            pl.BlockSpec((1, tile, D), lambda b, i: (b, i, 0)),
            pl.BlockSpec((1, tile, D_POOL), lambda b, i: (b, i, 0)),
            hg_spec, hg_spec, hg_spec,
        ],
        out_shape=[
            jax.ShapeDtypeStruct((B, T, D), F32),
            jax.ShapeDtypeStruct((B, T, D_POOL), F32),
            jax.ShapeDtypeStruct(hg_shape, BF16),
            jax.ShapeDtypeStruct(hg_shape, BF16),
            jax.ShapeDtypeStruct(hg_shape, BF16),
        ],
        scratch_shapes=[pltpu.VMEM((tile, D), BF16)],
        compiler_params=_params(2),
        name="inproj",
    )(x, ln_g, ln_b, w_in_bf16)


def _attend(q2, keys, vals, bias):
    m = q2.shape[0]
    first = lax.broadcasted_iota(jnp.int32, (m, PAIR), 1) < HEAD_DIM
    zero = jnp.zeros_like(q2)
    qs = jnp.concatenate([jnp.where(first, q2, zero), jnp.where(first, zero, q2)], axis=0)
    s = lax.dot_general(qs, keys, (((1,), (1,)), ((), ())), preferred_element_type=F32) + bias
    e = jnp.exp2(s - jnp.max(s, axis=-1, keepdims=True)).astype(BF16)
    vals_ones = jnp.concatenate([vals, jnp.ones_like(vals)], axis=1)
    o = jnp.dot(e, vals_ones, preferred_element_type=F32)
    o = o[:, :PAIR] / o[:, PAIR:]
    return jnp.where(first, o[:m], o[m:])


def _attn_body(q_ref, k_ref, v_ref, qm_ref, km_ref, vm_ref, bias_ref, y_ref, ym_ref, *, rows):
    rb = pl.program_id(2)
    n_real = KEY_ROWS * GRID_W
    pad = jnp.zeros((GRID_W - N_META, PAIR), BF16)

    def key_set(ref, meta_ref, k_off, ls):
        return jnp.concatenate(
            [ref[0, 0, pl.ds(k_off, n_real), ls], meta_ref[0, 0, :, ls], pad], axis=0)

    for blk in range(ROWS_PER_STEP // ROWS_PER_BLOCK):
        r0 = rb * ROWS_PER_STEP + blk * ROWS_PER_BLOCK
        ks = jnp.clip(r0 - KH // 2, 0, rows - KEY_ROWS)
        variant = jnp.where(r0 == 0, 0, jnp.where(r0 == rows - ROWS_PER_BLOCK, 2, 1))
        k_off = pl.multiple_of(ks * GRID_W, GRID_W)
        qs = slice(blk * BLOCK_Q, (blk + 1) * BLOCK_Q)
        for p in range(PAIRS_PER_GROUP):
            ls = slice(p * PAIR, (p + 1) * PAIR)
            out = _attend(q_ref[0, 0, qs, ls], key_set(k_ref, km_ref, k_off, ls),
                          key_set(v_ref, vm_ref, k_off, ls), bias_ref[variant, p])
            y_ref[0, qs, ls] = out.astype(BF16)

    @pl.when(rb == 0)
    def _():
        for p in range(PAIRS_PER_GROUP):
            ls = slice(p * PAIR, (p + 1) * PAIR)
            bias = jnp.concatenate(
                [jnp.broadcast_to(bias_ref[0, p, 0:1, :], (N_META, ATTN_KEYS)),
                 jnp.broadcast_to(bias_ref[0, p, BLOCK_Q:BLOCK_Q + 1, :], (N_META, ATTN_KEYS))],
                axis=0)
            out = _attend(qm_ref[0, 0, :, ls], key_set(k_ref, km_ref, 0, ls),
                          key_set(v_ref, vm_ref, 0, ls), bias)
            ym_ref[0, :, ls] = out.astype(BF16)


def _attention(q, k, v, qm, km, vm, bias_tab):
    B, _, T, _ = q.shape
    rows = T // GRID_W
    assert rows % ROWS_PER_STEP == 0 and rows >= KEY_ROWS + 1
    q_rows = ROWS_PER_STEP * GRID_W
    meta_spec = pl.BlockSpec((1, 1, N_META, HEAD_GROUP), lambda b, g, i: (0, g, 0, 0))
    kv_spec = pl.BlockSpec((1, 1, T, HEAD_GROUP), lambda b, g, i: (b, g, 0, 0))
    return pl.pallas_call(
        functools.partial(_attn_body, rows=rows),
        grid=(B, N_HEAD_GROUPS, rows // ROWS_PER_STEP),
        in_specs=[
            pl.BlockSpec((1, 1, q_rows, HEAD_GROUP), lambda b, g, i: (b, g, i, 0)),
            kv_spec, kv_spec,
            meta_spec, meta_spec, meta_spec,
            pl.BlockSpec((N_VARIANTS, PAIRS_PER_GROUP, 2 * BLOCK_Q, ATTN_KEYS),
                         lambda b, g, i: (0, g, 0, 0)),
        ],
        out_specs=[
            pl.BlockSpec((1, q_rows, HEAD_GROUP), lambda b, g, i: (b, i, g)),
            pl.BlockSpec((1, N_META, HEAD_GROUP), lambda b, g, i: (b, 0, g)),
        ],
        out_shape=[
            jax.ShapeDtypeStruct((B, T, D_ATTN), BF16),
            jax.ShapeDtypeStruct((B, N_META, D_ATTN), BF16),
        ],
        compiler_params=_params(3),
        name="attention",
    )(q, k, v, qm, km, vm, bias_tab)


def _bias_table(rpb, meta_bias):
    c = np.arange(GRID_W)
    kc = np.arange(GRID_W)
    cs = np.clip(c - KW // 2, 0, GRID_W - KW)
    valid = (kc[None, :] >= cs[:, None]) & (kc[None, :] < cs[:, None] + KW)
    dc = kc[None, :] - c[:, None] + (KW - 1)
    onehot = ((dc[None] == np.arange(2 * KW - 1)[:, None, None]) & valid[None]).astype(np.float32)
    toep = jnp.einsum('hrj,jck->hrck', rpb.astype(F32) * LOG2_E, onehot,
                      precision=lax.Precision.HIGHEST)
    toep = jnp.where(valid[None, None], toep, MASK_VALUE)
    toep = jnp.pad(toep, ((0, 0), (1, 1), (0, 0), (0, 0)), constant_values=MASK_VALUE)
    pairs = jnp.concatenate([toep[:, :-1], toep[:, 1:]], axis=-1)
    meta = jnp.pad(meta_bias.astype(F32) * LOG2_E, ((0, 0), (GRID_W, GRID_W - N_META)),
                   constant_values=MASK_VALUE).reshape(N_HEADS, 1, PAIR)
    heads_per_step = 4
    tab = pl.pallas_call(
        functools.partial(_bias_body, heads=heads_per_step),
        grid=(N_VARIANTS, N_HEADS // heads_per_step),
        in_specs=[
            pl.BlockSpec((heads_per_step, 2 * KH, GRID_W, PAIR), lambda v, g: (g, 0, 0, 0)),
            pl.BlockSpec((heads_per_step, 1, PAIR), lambda v, g: (g, 0, 0)),
        ],
        out_specs=pl.BlockSpec((1, heads_per_step, BLOCK_Q, ATTN_KEYS), lambda v, g: (v, g, 0, 0)),
        out_shape=jax.ShapeDtypeStruct((N_VARIANTS, N_HEADS, BLOCK_Q, ATTN_KEYS), F32),
        compiler_params=_params(2),
        name="bias_table",
    )(pairs, meta)
    return tab.reshape(N_VARIANTS, N_HEADS // 2, 2 * BLOCK_Q, ATTN_KEYS)


def _bias_body(pairs_ref, meta_ref, out_ref, *, heads):
    v = pl.program_id(0)
    offset = jnp.where(v == 0, 0, jnp.where(v == 1, KH // 2, KEY_ROWS - ROWS_PER_BLOCK))
    left = lax.broadcasted_iota(jnp.int32, (GRID_W, PAIR), 1) < GRID_W
    for h in range(heads):
        meta_tile = jnp.broadcast_to(meta_ref[h], (GRID_W, PAIR))
        for ri in range(ROWS_PER_BLOCK):
            lo = jnp.where(v == 0, 0, jnp.where(v == 1, ri, KH // 2 - 1))
            for j in range(ATTN_KEYS // PAIR):
                kr = 2 * j
                dr = kr - offset - ri + (KH - 1)
                tile = pairs_ref[h, jnp.clip(dr + 1, 0, 2 * KH - 1)]
                in_left = jnp.logical_and(kr >= lo, kr < lo + KH)
                in_right = jnp.logical_and(kr + 1 >= lo, kr + 1 < lo + KH)
                if kr + 1 < KEY_ROWS:
                    right_tile = jnp.where(in_right, tile, MASK_VALUE)
                else:
                    right_tile = meta_tile
                tile = jnp.where(left, jnp.where(in_left, tile, MASK_VALUE), right_tile)
                out_ref[0, h, ri * GRID_W:(ri + 1) * GRID_W, j * PAIR:(j + 1) * PAIR] = tile


def _pool_mixer(ext_ref, sum_ref, r0, n, p0, seq_len, wp_ref, ps_ref):
    p = p0 + lax.broadcasted_iota(jnp.int32, (n, 1), 0)
    outs = []
    for g, w in enumerate(POOL_WINDOWS):
        cs = slice(g * POOL_GROUP_DIM, (g + 1) * POOL_GROUP_DIM)
        span = n + 2 * HALO
        s = ext_ref[pl.ds(r0, span), cs]
        width = 1
        while width < w:
            s = s + pltpu.roll(s, span - width, axis=0)
            width *= 2
        sum_ref[:, cs] = s
        cnt = (jnp.minimum(p + w // 2, seq_len) - jnp.maximum(p - w // 2, 0)).astype(F32)
        m = sum_ref[pl.ds(HALO - w // 2, n), cs] / cnt - ext_ref[pl.ds(r0 + HALO, n), cs]
        y = jnp.dot(m.astype(BF16), wp_ref[g], preferred_element_type=F32) * ps_ref[:, cs]
        outs.append(y.astype(BF16))
    return jnp.concatenate(outs, axis=1)


def _mix_body(u_ref, up_ref, un_ref, um_ref, ya_ref, yam_ref, h_ref, hm_ref,
              wp_ref, ps_ref, wo_ref, g_ref, b_ref, h1_ref, h1m_ref,
              ext_ref, sum_ref, extm_ref, summ_ref, cat_ref, *, tile, seq_len):
    i = pl.program_id(1)
    last = pl.num_programs(1) - 1
    um = um_ref[0]
    prev = jnp.where(i == 0, um[N_META - HALO:], up_ref[0])
    nxt = jnp.where(i == last, jnp.zeros((HALO, D_POOL), F32), un_ref[0])
    ext_ref[...] = jnp.concatenate([prev, u_ref[0], nxt], axis=0)
    sub = tile // MIX_SUBTILES
    for r in range(MIX_SUBTILES):
        rows = slice(r * sub, (r + 1) * sub)
        cat_ref[rows, :D_POOL] = _pool_mixer(
            ext_ref, sum_ref.at[r], r * sub, sub, N_META + i * tile + r * sub, seq_len,
            wp_ref, ps_ref)
        cat_ref[rows, D_POOL:] = ya_ref[0, rows, :]
    for r in range(MIX_SUBTILES):
        rows = slice(r * sub, (r + 1) * sub)
        mix = jnp.dot(cat_ref[rows, :], wo_ref[...], preferred_element_type=F32)
        h1_ref[0, rows, :] = _layernorm(ALPHA * h_ref[0, rows, :] + mix, g_ref[...], b_ref[...])

    @pl.when(i == 0)
    def _():
        extm_ref[...] = jnp.concatenate(
            [jnp.zeros((HALO, D_POOL), F32), um, u_ref[0, :HALO, :]], axis=0)
        y_pool_m = _pool_mixer(extm_ref, summ_ref, 0, N_META, 0, seq_len, wp_ref, ps_ref)
        cat_m = jnp.concatenate([y_pool_m, yam_ref[0]], axis=1)
        mix_m = jnp.dot(cat_m, wo_ref[...], preferred_element_type=F32)
        h1m_ref[0] = _layernorm(ALPHA * hm_ref[0] + mix_m, g_ref[...], b_ref[...])


def _halo_specs(tile, n_tokens, width):
    per_tile = tile // HALO
    n_blocks = n_tokens // HALO
    prev = pl.BlockSpec((1, HALO, width),
                        lambda b, i, *_: (b, jnp.maximum(i * per_tile - 1, 0), 0))
    nxt = pl.BlockSpec((1, HALO, width),
                       lambda b, i, *_: (b, jnp.minimum((i + 1) * per_tile, n_blocks - 1), 0))
    return prev, nxt


def _mix(u, um, ya, yam, h, hm, w_pool_bf16, pool_scale, w_out_bf16, ln_g, ln_b, tile):
    B, T, D = h.shape
    prev_spec, next_spec = _halo_specs(tile, T, D_POOL)
    meta = lambda width: pl.BlockSpec((1, N_META, width), lambda b, i: (0, 0, 0))
    meta_b = lambda width: pl.BlockSpec((1, N_META, width), lambda b, i: (b, 0, 0))
    return pl.pallas_call(
        functools.partial(_mix_body, tile=tile, seq_len=N_META + T),
        grid=(B, T // tile),
        in_specs=[
            pl.BlockSpec((1, tile, D_POOL), lambda b, i: (b, i, 0)),
            prev_spec, next_spec,
            meta(D_POOL),
            pl.BlockSpec((1, tile, D_ATTN), lambda b, i: (b, i, 0)),
            meta_b(D_ATTN),
            pl.BlockSpec((1, tile, D), lambda b, i: (b, i, 0)),
            meta(D),
            _const_spec(w_pool_bf16.shape),
            _const_spec((1, D_POOL)),
            _const_spec((D, D)),
            _const_spec((1, D)),
            _const_spec((1, D)),
        ],
        out_specs=[
            pl.BlockSpec((1, tile, D), lambda b, i: (b, i, 0)),
            meta_b(D),
        ],
        out_shape=[
            jax.ShapeDtypeStruct((B, T, D), F32),
            jax.ShapeDtypeStruct((B, N_META, D), F32),
        ],
        scratch_shapes=[
            pltpu.VMEM((tile + 2 * HALO, D_POOL), F32),
            pltpu.VMEM((MIX_SUBTILES, tile // MIX_SUBTILES + 2 * HALO, D_POOL), F32),
            pltpu.VMEM((N_META + 2 * HALO, D_POOL), F32),
            pltpu.VMEM((N_META + 2 * HALO, D_POOL), F32),
            pltpu.VMEM((tile, D), BF16),
        ],
        compiler_params=_params(2),
        name="mix",
    )(u, u, u, um, ya, yam, h, hm, w_pool_bf16, pool_scale, w_out_bf16, ln_g, ln_b)


def _ffn_body(h1_ref, hp_ref, hn_ref, h1m_ref, wa_ref, wg_ref, par_ref, wd_ref,
              g_ref, b_ref, o_ref, lhs_ref, z_ref, acc_ref, *, tile):
    i = pl.program_id(1)
    c = pl.program_id(2)
    last_tile = pl.num_programs(1) - 1
    last_chunk = pl.num_programs(2) - 1
    n_sub = FF_CHUNK // FF_SUB

    @pl.when(c == 0)
    def _():
        prev = jnp.where(i == 0, h1m_ref[0, N_META - HALO:, :], hp_ref[0])
        lhs_ref[...] = jnp.concatenate([prev, h1_ref[0], hn_ref[0]], axis=0).astype(BF16)
        acc_ref[...] = jnp.zeros_like(acc_ref)

    is_seq_end = i == last_tile
    sqrt_half = np.sqrt(0.5).astype(np.float32)

    def conv(half, s, row0, n_rows):
        cs = slice(s * FF_SUB, (s + 1) * FF_SUB)
        z = z_ref.at[half * n_sub + s]
        taps = [par_ref[half, j:j + 1, cs] for j in range(3)]
        const = (taps[0] + taps[1] + taps[2]) * par_ref[half, 4:5, cs] + par_ref[half, 3:4, cs]
        return (z[pl.ds(HALO - 1 + row0, n_rows), :] * taps[0]
                + z[pl.ds(HALO + row0, n_rows), :] * taps[1]
                + z[pl.ds(HALO + 1 + row0, n_rows), :] * taps[2] + const)

    def gated(s, row0, n_rows):
        a = conv(0, s, row0, n_rows)
        g = conv(1, s, row0, n_rows)
        return (a * (0.5 * g * (1.0 + lax.erf(g * sqrt_half)))).astype(BF16)

    for s in range(n_sub):
        cs = slice(s * FF_SUB, (s + 1) * FF_SUB)
        for half, w_ref in enumerate((wa_ref, wg_ref)):
            z = jnp.dot(lhs_ref[...], w_ref[:, cs], preferred_element_type=F32)
            z_ref[half * n_sub + s] = z
            z_ref[half * n_sub + s, HALO + tile:, :] = jnp.where(
                is_seq_end, -par_ref[half, 4:5, cs], z[HALO + tile:, :])

    down = None
    for s in range(n_sub - 1):
        part = jnp.dot(gated(s, 0, tile), wd_ref[s * FF_SUB:(s + 1) * FF_SUB, :],
                       preferred_element_type=F32)
        down = part if down is None else down + part
    half_rows = tile // 2
    wd_last = wd_ref[(n_sub - 1) * FF_SUB:, :]
    for r in range(2):
        rows = slice(r * half_rows, (r + 1) * half_rows)
        part = jnp.dot(gated(n_sub - 1, r * half_rows, half_rows), wd_last,
                       preferred_element_type=F32)
        acc_ref[rows, :] += part if down is None else down[rows] + part

    @pl.when(c == last_chunk)
    def _():
        o_ref[0] = _layernorm(ALPHA * h1_ref[0] + acc_ref[...], g_ref[...], b_ref[...])


def _ffn(h1, h1m, w_up_p, ff_par, w_down_p, ln_g, ln_b, tile):
    B, T, D = h1.shape
    prev_spec, next_spec = _halo_specs(tile, T, D)
    const3 = lambda shape: pl.BlockSpec(shape, lambda b, i, c: (0,) * len(shape),
                                        pipeline_mode=pl.Buffered(1))
    n_par = ff_par.shape[1]
    return pl.pallas_call(
        functools.partial(_ffn_body, tile=tile),
        grid=(B, T // tile, N_FF_CHUNKS),
        in_specs=[
            pl.BlockSpec((1, tile, D), lambda b, i, c: (b, i, 0)),
            prev_spec, next_spec,
            pl.BlockSpec((1, N_META, D), lambda b, i, c: (b, 0, 0)),
            pl.BlockSpec((D, FF_CHUNK), lambda b, i, c: (0, c)),
            pl.BlockSpec((D, FF_CHUNK), lambda b, i, c: (0, N_FF_CHUNKS + c)),
            pl.BlockSpec((2, n_par, FF_CHUNK), lambda b, i, c: (0, 0, c)),
            pl.BlockSpec((FF_CHUNK, D), lambda b, i, c: (c, 0)),
            const3((1, D)), const3((1, D)),
        ],
        out_specs=pl.BlockSpec((1, tile, D), lambda b, i, c: (b, i, 0)),
        out_shape=jax.ShapeDtypeStruct((B, T, D), F32),
        scratch_shapes=[
            pltpu.VMEM((tile + 2 * HALO, D), BF16),
            pltpu.VMEM((2 * FF_CHUNK // FF_SUB, tile + 2 * HALO, FF_SUB), F32),
            pltpu.VMEM((tile, D), F32),
        ],
        compiler_params=_params(3),
        name="ffn",
    )(h1, h1, h1, h1m, w_up_p, w_up_p, ff_par, w_down_p, ln_g, ln_b)


def _pad_ff_cols(a):
    zeros = jnp.zeros(a.shape[:-1] + (D_FF_PAD - D_FF,), a.dtype)
    return jnp.concatenate([a[..., :D_FF], zeros, a[..., D_FF:], zeros], axis=-1)


def _w_up_body(w_ref, o_ref):
    zeros = jnp.zeros((w_ref.shape[0], D_FF_PAD - D_FF), BF16)
    o_ref[:, :D_FF] = w_ref[:, :D_FF].astype(BF16)
    o_ref[:, D_FF:D_FF_PAD] = zeros
    o_ref[:, D_FF_PAD:D_FF_PAD + D_FF] = w_ref[:, D_FF:].astype(BF16)
    o_ref[:, D_FF_PAD + D_FF:] = zeros


def _prep_w_up(w_up):
    D = w_up.shape[0]
    rows = 128
    return pl.pallas_call(
        _w_up_body,
        grid=(D // rows,),
        in_specs=[pl.BlockSpec((rows, 2 * D_FF), lambda i: (i, 0))],
        out_specs=pl.BlockSpec((rows, 2 * D_FF_PAD), lambda i: (i, 0)),
        out_shape=jax.ShapeDtypeStruct((D, 2 * D_FF_PAD), BF16),
        compiler_params=_params(1),
        name="prep_w_up",
    )(w_up)


def _split_halves(a):
    return jnp.transpose(a.reshape(a.shape[0], 2, D_FF_PAD), (1, 0, 2))


def _encode(x, meta, consts):
    (ln_in_g, ln_in_b, w_in, w_pool, pool_scale, bias_tab, w_out, ln1_g, ln1_b,
     w_up, ff_par, w_down, ln2_g, ln2_b) = consts
    hm, um, qm, km, vm = meta
    h, u, q, k, v = _inproj(x, ln_in_g, ln_in_b, w_in, TOKEN_TILE, INPROJ_SUBTILES)
    ya, yam = _attention(q, k, v, qm, km, vm, bias_tab)
    h1, h1m = _mix(u, um, ya, yam, h, hm, w_pool, pool_scale, w_out, ln1_g, ln1_b, TOKEN_TILE)
    return _ffn(h1, h1m, w_up, ff_par, w_down, ln2_g, ln2_b, TOKEN_TILE)


def kernel(x_prompt, x_sample, meta_tokens, ln_in_g, ln_in_b, w_in, w_pool, pool_scale, rpb,
           meta_bias, w_out, ln1_g, ln1_b, w_up, b_up, conv_w, conv_b, w_down, ln2_g, ln2_b):
    row = lambda a: a.reshape(1, -1).astype(F32)
    bias_tab = _bias_table(rpb[0], meta_bias[0])
    w_down_p = jnp.concatenate(
        [w_down[0].astype(BF16), jnp.zeros((D_FF_PAD - D_FF, D_MODEL), BF16)], axis=0)
    consts = (
        row(ln_in_g), row(ln_in_b), w_in[0].astype(BF16), w_pool[0].astype(BF16),
        row(pool_scale[0]), bias_tab, w_out[0].astype(BF16), row(ln1_g[0]), row(ln1_b[0]),
        _prep_w_up(w_up[0].astype(F32)),
        _split_halves(_pad_ff_cols(jnp.concatenate(
            [conv_w[0].astype(F32), row(conv_b[0]), row(b_up[0])], axis=0))),
        w_down_p, row(ln2_g[0]), row(ln2_b[0]),
    )
    meta = _inproj(meta_tokens[None].astype(F32), consts[0], consts[1], consts[2], N_META, 1)
    y_prompt = _encode(x_prompt, meta, consts)
    y_sample = _encode(x_sample, meta, consts)
    return (y_prompt, y_sample)
```
